```python
import math
import jax, jax.numpy as jnp
from jax import lax
import numpy as np

D_MODEL = 2048
BATCH = 4
SEQ = 4096
DEPTH = 2

MIX_W = D_MODEL // 2
M_HEADS = 4
M_DH = MIX_W // M_HEADS
M_CONV = 4
M_CHUNK = 128
DA_HEADS = 8
DA_DH = MIX_W // (2 * DA_HEADS)
SB_HEADS = 8
SB_DH = MIX_W // SB_HEADS
Q_BLOCK = 128
N_BRANCH = 3
D_FF = ((8 * D_MODEL // 3 + 255) // 256) * 256
FF_CONV = 3
N_MOD = 6
EPS = 1e-6
IN_SIZES = (MIX_W, MIX_W, MIX_W, MIX_W, M_HEADS, M_HEADS,
            MIX_W, MIX_W, MIX_W,
            MIX_W, MIX_W, MIX_W,
            D_MODEL, D_MODEL, D_MODEL)
N_IN = sum(IN_SIZES)

kernel_name = "hybrid_mlstm_diffattn_stickbreak_convffn"


def rmsnorm(x, g):
    xf = x.astype(jnp.float32)
    y = xf * lax.rsqrt(jnp.mean(xf * xf, axis=-1, keepdims=True) + EPS)
    return (y * g.astype(jnp.float32)).astype(x.dtype)


def causal_dwconv(x, w):
    k = w.shape[0]
    return lax.conv_general_dilated(
        x, w[:, None, :].astype(x.dtype), window_strides=(1,), padding=[(k - 1, 0)],
        dimension_numbers=('NWC', 'WIO', 'NWC'), feature_group_count=x.shape[-1])


def split_cols(z, sizes):
    outs, off = [], 0
    for s in sizes:
        outs.append(z[..., off:off + s])
        off += s
    return outs


def to_blocks(t, axis, blk):
    shp = t.shape
    t = t.reshape(shp[:axis] + (shp[axis] // blk, blk) + shp[axis + 1:])
    return jnp.moveaxis(t, axis, 0)


def mlstm_chunkwise(q, k, v, i_pre, f_pre):
    b_, h_, s_, d = q.shape
    q = q * (d ** -0.5)
    logf = jax.nn.log_sigmoid(f_pre)
    xs = (to_blocks(q, 2, M_CHUNK), to_blocks(k, 2, M_CHUNK), to_blocks(v, 2, M_CHUNK),
          to_blocks(i_pre, 2, M_CHUNK), to_blocks(logf, 2, M_CHUNK))
    causal = jnp.tril(jnp.ones((M_CHUNK, M_CHUNK), dtype=bool))

    def step(carry, inp):
        C, n, m = carry
        qc, kc, vc, ic, fc = inp
        b = jnp.cumsum(fc, axis=-1)
        dmat = jnp.where(causal, b[..., :, None] - b[..., None, :] + ic[..., None, :], -jnp.inf)
        m_inter = b + m[..., None]
        m_t = jnp.maximum(jnp.max(dmat, axis=-1), m_inter)
        w = jnp.einsum('bhtd,bhsd->bhts', qc, kc) * jnp.exp(dmat - m_t[..., None])
        decay = jnp.exp(m_inter - m_t)
        num = jnp.einsum('bhts,bhsd->bhtd', w, vc) + decay[..., None] * jnp.einsum('bhtd,bhde->bhte', qc, C)
        den = jnp.sum(w, axis=-1) + decay * jnp.einsum('bhtd,bhd->bht', qc, n)
        h = num / jnp.maximum(jnp.abs(den), jnp.exp(-m_t))[..., None]
        b_last = b[..., -1]
        g = b_last[..., None] - b + ic
        m_new = jnp.maximum(b_last + m, jnp.max(g, axis=-1))
        wk = jnp.exp(g - m_new[..., None])
        carry_decay = jnp.exp(b_last + m - m_new)
        C = carry_decay[..., None, None] * C + jnp.einsum('bhs,bhsd,bhse->bhde', wk, kc, vc)
        n = carry_decay[..., None] * n + jnp.einsum('bhs,bhsd->bhd', wk, kc)
        return (C, n, m_new), h

    init = (jnp.zeros((b_, h_, d, d), jnp.float32), jnp.zeros((b_, h_, d), jnp.float32),
            jnp.zeros((b_, h_), jnp.float32))
    _, hs = lax.scan(step, init, xs)
    return jnp.moveaxis(hs, 0, 2).reshape(b_, h_, s_, d)


def differential_attention(q, k, v, lam):
    b_, h_, _, s_, dh = q.shape
    qb = to_blocks(q, 3, Q_BLOCK)
    key_pos = jnp.arange(s_)

    def block(args):
        qi, idx = args
        q_pos = idx * Q_BLOCK + jnp.arange(Q_BLOCK)
        s = jnp.einsum('bhcqd,bhckd->bhcqk', qi, k).astype(jnp.float32)
        s = jnp.where(key_pos[None, :] <= q_pos[:, None], s, -jnp.inf)
        p = jax.nn.softmax(s, axis=-1)
        a = p[:, :, 0] - lam * p[:, :, 1]
        return jnp.einsum('bhqk,bhkd->bhqd', a.astype(v.dtype), v)

    out = lax.map(block, (qb, jnp.arange(s_ // Q_BLOCK)))
    return jnp.moveaxis(out, 0, 2).reshape(b_, h_, s_, v.shape[-1])


def stick_breaking_attention(q, k, v):
    b_, h_, s_, d = q.shape
    scale = d ** -0.5
    qb = to_blocks(q, 2, Q_BLOCK)
    key_pos = jnp.arange(s_)

    def block(args):
        qi, idx = args
        q_pos = idx * Q_BLOCK + jnp.arange(Q_BLOCK)
        z = jnp.einsum('bhqd,bhkd->bhqk', qi, k).astype(jnp.float32) * scale
        mask = key_pos[None, :] < q_pos[:, None]
        log_1m = jnp.where(mask, jax.nn.log_sigmoid(-z), 0.0)
        suffix = lax.cumsum(log_1m, axis=3, reverse=True) - log_1m
        a = jnp.where(mask, jnp.exp(jax.nn.log_sigmoid(z) + suffix), 0.0)
        return jnp.einsum('bhqk,bhkd->bhqd', a.astype(v.dtype), v)

    out = lax.map(block, (qb, jnp.arange(s_ // Q_BLOCK)))
    return jnp.moveaxis(out, 0, 2).reshape(b_, h_, s_, d)


def setup_inputs(seed: int = 0) -> dict:
    key = jax.random.key(seed)
    ks = jax.random.split(key, 24)
    nrm = jax.random.normal
    L, D = DEPTH, D_MODEL
    return {
        "x": nrm(ks[0], (BATCH, SEQ, D), jnp.float32),
        "c": nrm(ks[1], (BATCH, D), jnp.float32),
        "w_ada": nrm(ks[2], (L, D, N_MOD * D), jnp.float32) * (0.5 * D ** -0.5),
        "b_ada": nrm(ks[3], (L, N_MOD * D), jnp.float32) * 0.02,
        "g_mix": 1.0 + 0.02 * nrm(ks[4], (L, D), jnp.float32),
        "g_ffn": 1.0 + 0.02 * nrm(ks[5], (L, D), jnp.float32),
        "w_in": nrm(ks[6], (L, D, N_IN), jnp.float32) * D ** -0.5,
        "b_gate_if": jnp.concatenate([0.1 * nrm(ks[7], (L, M_HEADS), jnp.float32),
                                      3.0 + 0.5 * nrm(ks[8], (L, M_HEADS), jnp.float32)], axis=-1),
        "w_mconv": nrm(ks[9], (L, M_CONV, 2 * MIX_W), jnp.float32) * M_CONV ** -0.5,
        "g_mout": 1.0 + 0.02 * nrm(ks[10], (L, M_HEADS, M_DH), jnp.float32),
        "g_dq": 1.0 + 0.02 * nrm(ks[11], (L, DA_DH), jnp.float32),
        "g_dk": 1.0 + 0.02 * nrm(ks[12], (L, DA_DH), jnp.float32),
        "lam_q1": 0.1 * nrm(ks[13], (L, DA_DH), jnp.float32),
        "lam_k1": 0.1 * nrm(ks[14], (L, DA_DH), jnp.float32),
        "lam_q2": 0.1 * nrm(ks[15], (L, DA_DH), jnp.float32),
        "lam_k2": 0.1 * nrm(ks[16], (L, DA_DH), jnp.float32),
        "g_dsub": 1.0 + 0.02 * nrm(ks[17], (L, 2 * DA_DH), jnp.float32),
        "w_branch": nrm(ks[18], (L, N_BRANCH, MIX_W, D), jnp.float32) * MIX_W ** -0.5,
        "w_out": nrm(ks[19], (L, D, D), jnp.float32) * D ** -0.5,
        "w_up": nrm(ks[20], (L, D, 2 * D_FF), jnp.float32) * D ** -0.5,
        "w_ffconv": nrm(ks[21], (L, FF_CONV, D_FF), jnp.float32) * FF_CONV ** -0.5,
        "w_down": nrm(ks[22], (L, D_FF, D), jnp.float32) * D_FF ** -0.5,
    }


def reference(x, c, w_ada, b_ada, g_mix, g_ffn, w_in, b_gate_if, w_mconv, g_mout, g_dq, g_dk,
              lam_q1, lam_k1, lam_q2, lam_k2, g_dsub, w_branch, w_out, w_up, w_ffconv, w_down):
    B, S, D = x.shape
    for l in range(DEPTH):
        mod = (jax.nn.silu(c) @ w_ada[l] + b_ada[l]).reshape(B, N_MOD, 1, D)
        sh1, sc1, gt1, sh2, sc2, gt2 = (mod[:, j] for j in range(N_MOD))

        h = rmsnorm(x, g_mix[l]) * (1.0 + sc1) + sh1
        z = h @ w_in[l]
        (mq, mk, mv, mo, mi, mf, dq, dk, dv, sq, sk, sv,
         gm, gd, gs) = split_cols(z, IN_SIZES)

        mqk = jax.nn.silu(causal_dwconv(jnp.concatenate([mq, mk], axis=-1), w_mconv[l]))
        mq, mk = mqk[..., :MIX_W], mqk[..., MIX_W:]
        heads_m = lambda t: t.reshape(B, S, M_HEADS, M_DH).transpose(0, 2, 1, 3).astype(jnp.float32)
        i_pre = (mi + b_gate_if[l, :M_HEADS]).transpose(0, 2, 1).astype(jnp.float32)
        f_pre = (mf + b_gate_if[l, M_HEADS:]).transpose(0, 2, 1).astype(jnp.float32)
        hm = mlstm_chunkwise(heads_m(mq), heads_m(mk), heads_m(mv), i_pre, f_pre)
        hm = rmsnorm(hm.transpose(0, 2, 1, 3), g_mout[l]).astype(x.dtype)
        out_m = hm.reshape(B, S, MIX_W) * jax.nn.sigmoid(mo)

        lam_init = 0.8 - 0.6 * math.exp(-0.3 * l)
        lam = (jnp.exp(jnp.sum(lam_q1[l] * lam_k1[l])) - jnp.exp(jnp.sum(lam_q2[l] * lam_k2[l]))
               + lam_init).astype(jnp.float32)
        qd = rmsnorm(dq.reshape(B, S, DA_HEADS, 2, DA_DH), g_dq[l]) * (DA_DH ** -0.5)
        kd = rmsnorm(dk.reshape(B, S, DA_HEADS, 2, DA_DH), g_dk[l])
        vd = dv.reshape(B, S, DA_HEADS, 2 * DA_DH).transpose(0, 2, 1, 3)
        od = differential_attention(qd.transpose(0, 2, 3, 1, 4), kd.transpose(0, 2, 3, 1, 4), vd, lam)
        od = rmsnorm(od.transpose(0, 2, 1, 3), g_dsub[l]) * (1.0 - lam_init)
        out_d = od.reshape(B, S, MIX_W)

        heads_s = lambda t: t.reshape(B, S, SB_HEADS, SB_DH).transpose(0, 2, 1, 3)
        osb = stick_breaking_attention(heads_s(sq), heads_s(sk), heads_s(sv))
        out_s = osb.transpose(0, 2, 1, 3).reshape(B, S, MIX_W)

        merged = (jax.nn.sigmoid(gm) * (out_m @ w_branch[l, 0])
                  + jax.nn.sigmoid(gd) * (out_d @ w_branch[l, 1])
                  + jax.nn.sigmoid(gs) * (out_s @ w_branch[l, 2]))
        x = x + gt1 * (merged @ w_out[l])

        h = rmsnorm(x, g_ffn[l]) * (1.0 + sc2) + sh2
        hid = h @ w_up[l]
        gate = causal_dwconv(hid[..., :D_FF], w_ffconv[l])
        x = x + gt2 * ((jax.nn.silu(gate) * hid[..., D_FF:]) @ w_down[l])
    return x
```

```python
import functools
import math

import jax
import jax.numpy as jnp
from jax import lax
from jax.experimental import pallas as pl
from jax.experimental.pallas import tpu as pltpu

EPS = 1e-6
M_HEADS = 4
M_CONV = 4
M_CHUNK = 128
DA_HEADS = 8
SB_HEADS = 8
N_MOD = 6
FF_CONV = 3
LANE = 128
CONV_HALO = 8
VMEM_LIMIT_BYTES = 56 * 1024 * 1024
HIGHEST = lax.Precision.HIGHEST
NEG_INF = float("-inf")

BF16 = jnp.bfloat16
F32 = jnp.float32


def _params(*sem):
    return pltpu.CompilerParams(dimension_semantics=sem, vmem_limit_bytes=VMEM_LIMIT_BYTES)


def _dot(a, b):
    return jnp.dot(a, b, preferred_element_type=F32)


def _dot_nt(a, b):
    return lax.dot_general(a, b, (((1,), (1,)), ((), ())), preferred_element_type=F32)


def _dot_tn(a, b):
    return lax.dot_general(a, b, (((0,), (0,)), ((), ())), preferred_element_type=F32)


def _silu(x):
    return x * jax.nn.sigmoid(x)


def _softplus(x):
    return jnp.maximum(x, 0.0) + jnp.log1p(jnp.exp(-jnp.abs(x)))


def _log_sigmoid(x):
    return -_softplus(-x)


def _ada_kernel(c_ref, w_ref, b_ref, o_ref):
    c = c_ref[...]
    o_ref[...] = jnp.dot(_silu(c), w_ref[...], preferred_element_type=F32,
                         precision=HIGHEST) + b_ref[...]


def _ada_mod(c, w_ada, b_ada):
    n_layers, d, n = w_ada.shape
    b = c.shape[0]
    rows = 8
    c_pad = jnp.zeros((rows, d), F32).at[:b].set(c)
    tn = 1024
    out = pl.pallas_call(
        _ada_kernel,
        grid=(n_layers, n // tn),
        in_specs=[
            pl.BlockSpec((rows, d), lambda l, j: (0, 0)),
            pl.BlockSpec((None, d, tn), lambda l, j: (l, 0, j)),
            pl.BlockSpec((None, 1, tn), lambda l, j: (l, 0, j)),
        ],
        out_specs=pl.BlockSpec((None, rows, tn), lambda l, j: (l, 0, j)),
        out_shape=jax.ShapeDtypeStruct((n_layers, rows, n), F32),
        compiler_params=_params("parallel", "parallel"),
        name="ada_mod",
    )(c_pad, w_ada, b_ada.reshape(n_layers, 1, n))
    return out[:, :b].reshape(n_layers, b, N_MOD, 1, d)


def _norm_mod_kernel(x_ref, g_ref, sc_ref, sh_ref, o_ref):
    x = x_ref[...]
    y = x * lax.rsqrt(jnp.mean(x * x, axis=-1, keepdims=True) + EPS) * g_ref[...]
    o_ref[...] = (y * (1.0 + sc_ref[...]) + sh_ref[...]).astype(o_ref.dtype)


def _norm_mod(x2, g, mod, j_scale, j_shift, seq):
    t, d = x2.shape
    ts = min(512, seq)
    per_b = seq // ts
    return pl.pallas_call(
        _norm_mod_kernel,
        grid=(t // ts,),
        in_specs=[
            pl.BlockSpec((ts, d), lambda i: (i, 0)),
            pl.BlockSpec((1, d), lambda i: (0, 0)),
            pl.BlockSpec((None, None, 1, d), lambda i: (i // per_b, j_scale, 0, 0)),
            pl.BlockSpec((None, None, 1, d), lambda i: (i // per_b, j_shift, 0, 0)),
        ],
        out_specs=pl.BlockSpec((ts, d), lambda i: (i, 0)),
        out_shape=jax.ShapeDtypeStruct((t, d), BF16),
        compiler_params=_params("parallel"),
        name="norm_mod",
    )(x2, g.reshape(1, d), mod, mod)


def _mm_kernel(a_ref, w_ref, o_ref):
    o_ref[...] = _dot(a_ref[...], w_ref[...]).astype(o_ref.dtype)


def _matmul(a, w, out_dtype, tm, tn, name):
    t, k = a.shape
    n = w.shape[1]
    tm = min(tm, t)
    tn = min(tn, n)
    return pl.pallas_call(
        _mm_kernel,
        grid=(t // tm, n // tn),
        in_specs=[
            pl.BlockSpec((tm, k), lambda i, j: (i, 0)),
            pl.BlockSpec((k, tn), lambda i, j: (0, j)),
        ],
        out_specs=pl.BlockSpec((tm, tn), lambda i, j: (i, j)),
        out_shape=jax.ShapeDtypeStruct((t, n), out_dtype),
        compiler_params=_params("parallel", "parallel"),
        name=name,
    )(a, w)


def _mm_bias_kernel(a_ref, w_ref, b_ref, o_ref):
    o_ref[...] = _dot(a_ref[...], w_ref[...]) + b_ref[...]


def _matmul_bias_f32(a, w, bias, tm, name):
    t, k = a.shape
    n = w.shape[1]
    tm = min(tm, t)
    return pl.pallas_call(
        _mm_bias_kernel,
        grid=(t // tm,),
        in_specs=[
            pl.BlockSpec((tm, k), lambda i: (i, 0)),
            pl.BlockSpec((k, n), lambda i: (0, 0)),
            pl.BlockSpec((1, n), lambda i: (0, 0)),
        ],
        out_specs=pl.BlockSpec((tm, n), lambda i: (i, 0)),
        out_shape=jax.ShapeDtypeStruct((t, n), F32),
        compiler_params=_params("parallel"),
        name=name,
    )(a, w, bias)


def _mlstm_kernel(q_ref, k_ref, v_ref, o_ref, zif_ref, wconv_ref, g_ref, out_ref,
                  c_scr, n_scr, m_scr, qtail_scr, ktail_scr, *, dh):
    chunk = q_ref.shape[0]
    width = q_ref.shape[1]
    heads = width // dh

    @pl.when(pl.program_id(1) == 0)
    def _():
        c_scr[...] = jnp.zeros_like(c_scr)
        n_scr[...] = jnp.zeros_like(n_scr)
        m_scr[...] = jnp.zeros_like(m_scr)
        qtail_scr[...] = jnp.zeros_like(qtail_scr)
        ktail_scr[...] = jnp.zeros_like(ktail_scr)

    wconv = wconv_ref[...]

    def conv_silu(raw, tail_scr, w):
        xp = jnp.concatenate([tail_scr[...], raw], axis=0)
        acc = raw * w[M_CONV - 1:M_CONV]
        for back in range(1, M_CONV):
            shifted = pltpu.roll(xp, back, 0)[CONV_HALO:]
            acc = acc + shifted * w[M_CONV - 1 - back:M_CONV - back]
        tail_scr[...] = raw[chunk - CONV_HALO:]
        return _silu(acc)

    q_all = conv_silu(q_ref[...].astype(F32), qtail_scr, wconv[:, :width]) * (dh ** -0.5)
    k_all = conv_silu(k_ref[...].astype(F32), ktail_scr, wconv[:, width:])

    zif = zif_ref[...]
    lsf = _log_sigmoid(zif)
    zif_t = zif.T
    lsf_t = lsf.T

    row = lax.broadcasted_iota(jnp.int32, (chunk, chunk), 0)
    col = lax.broadcasted_iota(jnp.int32, (chunk, chunk), 1)
    causal = col <= row
    tri = causal.astype(F32)
    tri_t = (row <= col).astype(F32)

    for h in range(heads):
        sl = slice(h * dh, (h + 1) * dh)
        qh = q_all[:, sl]
        kh = k_all[:, sl]
        qb = qh.astype(BF16)
        kb = kh.astype(BF16)
        vb = v_ref[:, sl]

        i_col = zif[:, h:h + 1]
        i_row = zif_t[h:h + 1, :]
        lf_col = lsf[:, heads + h:heads + h + 1]
        lf_row = lsf_t[heads + h:heads + h + 1, :]

        b_colb = jnp.dot(tri, jnp.broadcast_to(lf_col, (chunk, chunk)),
                         preferred_element_type=F32, precision=HIGHEST)
        b_rowb = jnp.dot(jnp.broadcast_to(lf_row, (chunk, chunk)), tri_t,
                         preferred_element_type=F32, precision=HIGHEST)
        b_col = b_colb[:, :1]
        b_row = b_rowb[:1, :]
        b_last = b_row[:, chunk - 1:chunk]

        m_prev = m_scr[h][:, :1]
        dmat = jnp.where(causal, b_colb - b_rowb + i_row, NEG_INF)
        m_inter = b_col + m_prev
        m_t = jnp.maximum(jnp.max(dmat, axis=-1, keepdims=True), m_inter)
        w = _dot_nt(qb, kb) * jnp.exp(dmat - m_t)
        decay = jnp.exp(m_inter - m_t)
        c_prev = c_scr[h]
        n_prev = n_scr[h]
        num = _dot(w.astype(BF16), vb) + decay * _dot(qb, c_prev.astype(BF16))
        den = jnp.sum(w, axis=-1, keepdims=True) + decay * jnp.sum(qh * n_prev, axis=-1, keepdims=True)
        hh = num / jnp.maximum(jnp.abs(den), jnp.exp(-m_t))

        hn = hh * lax.rsqrt(jnp.mean(hh * hh, axis=-1, keepdims=True) + EPS) * g_ref[:, sl]
        out_ref[:, sl] = (hn * jax.nn.sigmoid(o_ref[:, sl].astype(F32))).astype(out_ref.dtype)

        g_col = b_last - b_col + i_col
        m_new = jnp.maximum(b_last + m_prev, jnp.max(g_col, axis=0, keepdims=True))
        wk = jnp.exp(g_col - m_new)
        carry_decay = jnp.exp(b_last + m_prev - m_new)
        kw = kh * wk
        c_scr[h] = carry_decay * c_prev + _dot_tn(kw.astype(BF16), vb)
        n_scr[h] = carry_decay * n_prev + jnp.sum(kw, axis=0, keepdims=True)
        m_scr[h] = jnp.broadcast_to(m_new, m_scr.shape[1:])


def _mlstm(zmain, zif, w_mconv, g_mout, batch, seq, width):
    t = zmain.shape[0]
    dh = width // M_HEADS
    nc = seq // M_CHUNK
    row_map = lambda cb: (lambda b, c: (b * nc + c, cb))
    return pl.pallas_call(
        functools.partial(_mlstm_kernel, dh=dh),
        grid=(batch, nc),
        in_specs=[
            pl.BlockSpec((M_CHUNK, width), row_map(0)),
            pl.BlockSpec((M_CHUNK, width), row_map(1)),
            pl.BlockSpec((M_CHUNK, width), row_map(2)),
            pl.BlockSpec((M_CHUNK, width), row_map(3)),
            pl.BlockSpec((M_CHUNK, LANE), row_map(0)),
            pl.BlockSpec((M_CONV, 2 * width), lambda b, c: (0, 0)),
            pl.BlockSpec((1, width), lambda b, c: (0, 0)),
        ],
        out_specs=pl.BlockSpec((M_CHUNK, width), row_map(0)),
        out_shape=jax.ShapeDtypeStruct((t, width), BF16),
        scratch_shapes=[
            pltpu.VMEM((M_HEADS, dh, dh), F32),
            pltpu.VMEM((M_HEADS, 1, dh), F32),
            pltpu.VMEM((M_HEADS, 1, LANE), F32),
            pltpu.VMEM((CONV_HALO, width), F32),
            pltpu.VMEM((CONV_HALO, width), F32),
        ],
        compiler_params=_params("parallel", "arbitrary"),
        name="mlstm",
    )(zmain, zmain, zmain, zmain, zif, w_mconv, g_mout.reshape(1, width))


def _qk_norm_kernel(q_ref, k_ref, gq_ref, gk_ref, qo_ref, ko_ref, *, dh):
    width = q_ref.shape[1]
    row = lax.broadcasted_iota(jnp.int32, (LANE, LANE), 0)
    col = lax.broadcasted_iota(jnp.int32, (LANE, LANE), 1)
    same_group = ((row // dh) == (col // dh)).astype(BF16)

    def group_rms(x_ref, g_ref, o_ref, scale):
        for tile in range(width // LANE):
            sl = slice(tile * LANE, (tile + 1) * LANE)
            x = x_ref[:, sl].astype(F32)
            sq = x * x
            hi = sq.astype(BF16)
            lo = (sq - hi.astype(F32)).astype(BF16)
            ssum = _dot(hi, same_group) + _dot(lo, same_group)
            y = x * lax.rsqrt(ssum * (1.0 / dh) + EPS) * g_ref[:, sl]
            o_ref[:, sl] = (y * scale).astype(o_ref.dtype)

    group_rms(q_ref, gq_ref, qo_ref, dh ** -0.5)
    group_rms(k_ref, gk_ref, ko_ref, 1.0)


def _qk_norm(zmain, g_dq, g_dk, width, col_q, col_k, seq):
    t = zmain.shape[0]
    dh = g_dq.shape[0]
    ts = min(512, seq)
    reps = width // dh
    return pl.pallas_call(
        functools.partial(_qk_norm_kernel, dh=dh),
        grid=(t // ts,),
        in_specs=[
            pl.BlockSpec((ts, width), lambda i: (i, col_q)),
            pl.BlockSpec((ts, width), lambda i: (i, col_k)),
            pl.BlockSpec((1, width), lambda i: (0, 0)),
            pl.BlockSpec((1, width), lambda i: (0, 0)),
        ],
        out_specs=[pl.BlockSpec((ts, width), lambda i: (i, 0)),
                   pl.BlockSpec((ts, width), lambda i: (i, 0))],
        out_shape=[jax.ShapeDtypeStruct((t, width), BF16), jax.ShapeDtypeStruct((t, width), BF16)],
        compiler_params=_params("parallel"),
        name="qk_norm",
    )(zmain, zmain, jnp.tile(g_dq, reps).reshape(1, width), jnp.tile(g_dk, reps).reshape(1, width))


def _diff_attn_kernel(q_ref, k_ref, v_ref, lq1_ref, lk1_ref, lq2_ref, lk2_ref, g_ref, out_ref,
                      m_scr, l_scr, acc_scr, *, lam_init, dh):
    tq = q_ref.shape[0]
    i = pl.program_id(2)
    q = q_ref[...]
    lane = lax.broadcasted_iota(jnp.int32, q.shape, 1)
    zero = jnp.zeros_like(q)
    q_halves = (jnp.where(lane < dh, q, zero), jnp.where(lane >= dh, q, zero))

    m_scr[...] = jnp.full_like(m_scr, NEG_INF)
    l_scr[...] = jnp.zeros_like(l_scr)
    acc_scr[...] = jnp.zeros_like(acc_scr)

    def block(j, masked):
        start = pl.multiple_of(j * tq, tq)
        k = k_ref[pl.ds(start, tq), :]
        v = v_ref[pl.ds(start, tq), :]
        for c in range(2):
            s = _dot_nt(q_halves[c], k)
            if masked:
                row = lax.broadcasted_iota(jnp.int32, s.shape, 0)
                col = lax.broadcasted_iota(jnp.int32, s.shape, 1)
                s = jnp.where(col <= row, s, NEG_INF)
            m_prev = m_scr[c]
            m_new = jnp.maximum(m_prev, jnp.max(s, axis=-1, keepdims=True))
            alpha = jnp.exp(m_prev - m_new)
            p = jnp.exp(s - m_new)
            l_scr[c] = alpha * l_scr[c] + jnp.sum(p, axis=-1, keepdims=True)
            acc_scr[c] = alpha * acc_scr[c] + _dot(p.astype(BF16), v)
            m_scr[c] = m_new

    def body(j, carry):
        block(j, False)
        return carry

    lax.fori_loop(0, i, body, 0)
    block(i, True)

    lam = (jnp.exp(jnp.sum(lq1_ref[...] * lk1_ref[...], axis=-1, keepdims=True))
           - jnp.exp(jnp.sum(lq2_ref[...] * lk2_ref[...], axis=-1, keepdims=True)) + lam_init)
    o = acc_scr[0] / l_scr[0] - lam * (acc_scr[1] / l_scr[1])
    on = o * lax.rsqrt(jnp.mean(o * o, axis=-1, keepdims=True) + EPS) * g_ref[...]
    out_ref[...] = (on * (1.0 - lam_init)).astype(out_ref.dtype)


def _diff_attn(qn, kn, zmain, col_v, lam_params, g_dsub, lam_init, batch, seq, width):
    t = qn.shape[0]
    dh = lam_params[0].shape[0]
    hw = 2 * dh
    heads = width // hw
    tq = min(512, seq)
    nq = seq // tq
    v_col0 = col_v * (width // hw)
    lam_specs = [pl.BlockSpec((1, dh), lambda b, h, i: (0, 0)) for _ in range(4)]
    return pl.pallas_call(
        functools.partial(_diff_attn_kernel, lam_init=lam_init, dh=dh),
        grid=(batch, heads, nq),
        in_specs=[
            pl.BlockSpec((tq, hw), lambda b, h, i: (b * nq + i, h)),
            pl.BlockSpec((seq, hw), lambda b, h, i: (b, h)),
            pl.BlockSpec((seq, hw), lambda b, h, i: (b, v_col0 + h)),
            *lam_specs,
            pl.BlockSpec((1, hw), lambda b, h, i: (0, 0)),
        ],
        out_specs=pl.BlockSpec((tq, hw), lambda b, h, i: (b * nq + i, h)),
        out_shape=jax.ShapeDtypeStruct((t, width), BF16),
        scratch_shapes=[
            pltpu.VMEM((2, tq, 1), F32),
            pltpu.VMEM((2, tq, 1), F32),
            pltpu.VMEM((2, tq, hw), F32),
        ],
        compiler_params=_params("parallel", "parallel", "arbitrary"),
        name="diff_attn",
    )(qn, kn, zmain, *[p.reshape(1, dh) for p in lam_params], g_dsub.reshape(1, hw))


def _stick_kernel(q_ref, k_ref, v_ref, out_ref, carry_scr, acc_scr, *, kb):
    tq, dh = q_ref.shape
    i = pl.program_id(2)
    scale = dh ** -0.5
    q = q_ref[...]
    diag_blocks = tq // kb

    jrow = lax.broadcasted_iota(jnp.int32, (kb, kb), 0)
    scol = lax.broadcasted_iota(jnp.int32, (kb, kb), 1)
    later = (jrow > scol).astype(BF16)

    carry_scr[...] = jnp.zeros_like(carry_scr)
    acc_scr[...] = jnp.zeros_like(acc_scr)

    def block(jb, masked):
        start = pl.multiple_of(jb * kb, kb)
        k = k_ref[pl.ds(start, kb), :]
        v = v_ref[pl.ds(start, kb), :]
        z = _dot_nt(q, k) * scale
        log_1m = -_softplus(z)
        if masked:
            q_pos = i * tq + lax.broadcasted_iota(jnp.int32, z.shape, 0)
            k_pos = jb * kb + lax.broadcasted_iota(jnp.int32, z.shape, 1)
            mask = k_pos < q_pos
            log_1m = jnp.where(mask, log_1m, 0.0)
        hi = log_1m.astype(BF16)
        lo = (log_1m - hi.astype(F32)).astype(BF16)
        suffix = _dot(hi, later) + _dot(lo, later) + carry_scr[...]
        a = jnp.exp(z + log_1m + suffix)
        if masked:
            a = jnp.where(mask, a, 0.0)
        acc_scr[...] += _dot(a.astype(BF16), v)
        carry_scr[...] += jnp.sum(log_1m, axis=-1, keepdims=True)

    for d in range(diag_blocks):
        block((i + 1) * diag_blocks - 1 - d, True)

    def body(step, carry):
        block(i * diag_blocks - 1 - step, False)
        return carry

    lax.fori_loop(0, i * diag_blocks, body, 0)
    out_ref[...] = acc_scr[...].astype(out_ref.dtype)


def _stick_breaking(zmain, col_q, col_k, col_v, batch, seq, width):
    t = zmain.shape[0]
    dh = width // SB_HEADS
    tq = min(256, seq)
    nq = seq // tq
    per = width // dh
    return pl.pallas_call(
        functools.partial(_stick_kernel, kb=LANE),
        grid=(batch, SB_HEADS, nq),
        in_specs=[
            pl.BlockSpec((tq, dh), lambda b, h, i: (b * nq + i, col_q * per + h)),
            pl.BlockSpec((seq, dh), lambda b, h, i: (b, col_k * per + h)),
            pl.BlockSpec((seq, dh), lambda b, h, i: (b, col_v * per + h)),
        ],
        out_specs=pl.BlockSpec((tq, dh), lambda b, h, i: (b * nq + i, h)),
        out_shape=jax.ShapeDtypeStruct((t, width), BF16),
        scratch_shapes=[
            pltpu.VMEM((tq, 1), F32),
            pltpu.VMEM((tq, dh), F32),
        ],
        compiler_params=_params("parallel", "parallel", "arbitrary"),
        name="stick_breaking",
    )(zmain, zmain, zmain)


def _merge_kernel(om_ref, od_ref, os_ref, wm_ref, wd_ref, ws_ref, gm_ref, gd_ref, gs_ref, o_ref):
    acc = jax.nn.sigmoid(gm_ref[...].astype(F32)) * _dot(om_ref[...], wm_ref[...])
    acc = acc + jax.nn.sigmoid(gd_ref[...].astype(F32)) * _dot(od_ref[...], wd_ref[...])
    acc = acc + jax.nn.sigmoid(gs_ref[...].astype(F32)) * _dot(os_ref[...], ws_ref[...])
    o_ref[...] = acc.astype(o_ref.dtype)


def _merge(out_m, out_d, out_s, w_branch, zmain, gate_col0, d_model):
    t, width = out_m.shape
    tm = min(1024, t)
    tn = 1024
    nj = d_model // tn
    act_spec = pl.BlockSpec((tm, width), lambda i, j: (i, 0))
    w_spec = lambda br: pl.BlockSpec((None, width, tn), lambda i, j: (br, 0, j))
    gate_spec = lambda br: pl.BlockSpec((tm, tn), lambda i, j: (i, gate_col0 + br * nj + j))
    return pl.pallas_call(
        _merge_kernel,
        grid=(t // tm, nj),
        in_specs=[act_spec, act_spec, act_spec, w_spec(0), w_spec(1), w_spec(2),
                  gate_spec(0), gate_spec(1), gate_spec(2)],
        out_specs=pl.BlockSpec((tm, tn), lambda i, j: (i, j)),
        out_shape=jax.ShapeDtypeStruct((t, d_model), BF16),
        compiler_params=_params("parallel", "parallel"),
        name="branch_merge",
    )(out_m, out_d, out_s, w_branch, w_branch, w_branch, zmain, zmain, zmain)


def _proj_residual_kernel(a_ref, w_ref, x_ref, gt_ref, o_ref):
    o_ref[...] = x_ref[...] + gt_ref[...] * _dot(a_ref[...], w_ref[...])


def _proj_residual(a, w, x2, mod, j_gate, seq, tm, tn, name):
    t, k = a.shape
    n = w.shape[1]
    tm = min(tm, seq)
    per_b = seq // tm
    return pl.pallas_call(
        _proj_residual_kernel,
        grid=(t // tm, n // tn),
        in_specs=[
            pl.BlockSpec((tm, k), lambda i, j: (i, 0)),
            pl.BlockSpec((k, tn), lambda i, j: (0, j)),
            pl.BlockSpec((tm, tn), lambda i, j: (i, j)),
            pl.BlockSpec((None, None, 1, tn), lambda i, j: (i // per_b, j_gate, 0, j)),
        ],
        out_specs=pl.BlockSpec((tm, tn), lambda i, j: (i, j)),
        out_shape=jax.ShapeDtypeStruct((t, n), F32),
        compiler_params=_params("parallel", "parallel"),
        name=name,
    )(a, w, x2, mod)


def _ffn_up_kernel(h_ref, wg_ref, wu_ref, wc_ref, o_ref, tail_scr, *, tiles_per_seq):
    tm = h_ref.shape[0]

    @pl.when(pl.program_id(1) % tiles_per_seq == 0)
    def _():
        tail_scr[...] = jnp.zeros_like(tail_scr)

    h = h_ref[...]
    gate = _dot(h, wg_ref[...])
    up = _dot(h, wu_ref[...])
    wc = wc_ref[...]
    xp = jnp.concatenate([tail_scr[...], gate], axis=0)
    conv = gate * wc[FF_CONV - 1:FF_CONV]
    for back in range(1, FF_CONV):
        conv = conv + pltpu.roll(xp, back, 0)[CONV_HALO:] * wc[FF_CONV - 1 - back:FF_CONV - back]
    tail_scr[...] = gate[tm - CONV_HALO:]
    o_ref[...] = (_silu(conv) * up).astype(o_ref.dtype)


def _ffn_up(h2, w_up, w_ffconv, seq):
    t, k = h2.shape
    d_ff = w_ffconv.shape[1]
    tm = min(1024, seq)
    tn = 512
    nj = d_ff // tn
    return pl.pallas_call(
        functools.partial(_ffn_up_kernel, tiles_per_seq=seq // tm),
        grid=(nj, t // tm),
        in_specs=[
            pl.BlockSpec((tm, k), lambda j, i: (i, 0)),
            pl.BlockSpec((k, tn), lambda j, i: (0, j)),
            pl.BlockSpec((k, tn), lambda j, i: (0, nj + j)),
            pl.BlockSpec((FF_CONV, tn), lambda j, i: (0, j)),
        ],
        out_specs=pl.BlockSpec((tm, tn), lambda j, i: (i, j)),
        out_shape=jax.ShapeDtypeStruct((t, d_ff), BF16),
        scratch_shapes=[pltpu.VMEM((CONV_HALO, tn), F32)],
        compiler_params=_params("parallel", "arbitrary"),
        name="ffn_up",
    )(h2, w_up, w_up, w_ffconv)


def kernel(x, c, w_ada, b_ada, g_mix, g_ffn, w_in, b_gate_if, w_mconv, g_mout, g_dq, g_dk,
           lam_q1, lam_k1, lam_q2, lam_k2, g_dsub, w_branch, w_out, w_up, w_ffconv, w_down):
    batch, seq, d = x.shape
    depth = w_ada.shape[0]
    width = d // 2
    t = batch * seq
    n_gate = 2 * M_HEADS

    mod_all = _ada_mod(c, w_ada, b_ada)
    x2 = x.reshape(t, d)

    for l in range(depth):
        mod = mod_all[l]
        w_l = w_in[l]
        gate0 = 4 * width
        w_main = jnp.concatenate([w_l[:, :gate0], w_l[:, gate0 + n_gate:]], axis=1).astype(BF16)
        w_if = jnp.zeros((d, LANE), BF16).at[:, :n_gate].set(w_l[:, gate0:gate0 + n_gate].astype(BF16))
        b_if = jnp.zeros((1, LANE), F32).at[0, :n_gate].set(b_gate_if[l])

        h = _norm_mod(x2, g_mix[l], mod, 1, 0, seq)
        zmain = _matmul(h, w_main, BF16, 1024, 1024, "in_proj")
        zif = _matmul_bias_f32(h, w_if, b_if, 1024, "gate_proj")

        out_m = _mlstm(zmain, zif, w_mconv[l], g_mout[l], batch, seq, width)
        qn, kn = _qk_norm(zmain, g_dq[l], g_dk[l], width, 4, 5, seq)
        lam_init = 0.8 - 0.6 * math.exp(-0.3 * l)
        out_d = _diff_attn(qn, kn, zmain, 6, (lam_q1[l], lam_k1[l], lam_q2[l], lam_k2[l]),
                           g_dsub[l], lam_init, batch, seq, width)
        out_s = _stick_breaking(zmain, 7, 8, 9, batch, seq, width)

        merged = _merge(out_m, out_d, out_s, w_branch[l].astype(BF16), zmain, 10 * width // 1024, d)
        x2 = _proj_residual(merged, w_out[l].astype(BF16), x2, mod, 2, seq, 1024, 1024, "out_proj")

        h2 = _norm_mod(x2, g_ffn[l], mod, 4, 3, seq)
        act = _ffn_up(h2, w_up[l].astype(BF16), w_ffconv[l], seq)
        x2 = _proj_residual(act, w_down[l].astype(BF16), x2, mod, 5, seq, 512, 512, "ffn_down")

    return x2.reshape(batch, seq, d)
```

```python
import functools
import math

import jax
import jax.numpy as jnp
from jax import lax
from jax.experimental import pallas as pl
from jax.experimental.pallas import tpu as pltpu

EPS = 1e-6
M_HEADS = 4
M_CONV = 4
M_CHUNK = 128
DA_HEADS = 8
SB_HEADS = 8
N_MOD = 6
FF_CONV = 3
LANE = 128
CONV_HALO = 8
VMEM_LIMIT_BYTES = 56 * 1024 * 1024
HIGHEST = lax.Precision.HIGHEST
NEG_INF = float("-inf")
LOG2E = math.log2(math.e)
STICK_DEAD_LOG2 = 200.0

BF16 = jnp.bfloat16
F32 = jnp.float32


def _params(*sem):
    return pltpu.CompilerParams(dimension_semantics=sem, vmem_limit_bytes=VMEM_LIMIT_BYTES)


def _dot(a, b):
    return jnp.dot(a, b, preferred_element_type=F32)


def _dot_nt(a, b):
    return lax.dot_general(a, b, (((1,), (1,)), ((), ())), preferred_element_type=F32)


def _dot_tn(a, b):
    return lax.dot_general(a, b, (((0,), (0,)), ((), ())), preferred_element_type=F32)


def _silu(x):
    return x * jax.nn.sigmoid(x)


def _softplus(x):
    return jnp.maximum(x, 0.0) + jnp.log1p(jnp.exp(-jnp.abs(x)))


def _log_sigmoid(x):
    return -_softplus(-x)


def _ada_kernel(c_ref, w_ref, b_ref, o_ref):
    c = c_ref[...]
    o_ref[...] = jnp.dot(_silu(c), w_ref[...], preferred_element_type=F32,
                         precision=HIGHEST) + b_ref[...]


def _ada_mod(c, w_ada, b_ada):
    n_layers, d, n = w_ada.shape
    b = c.shape[0]
    rows = 8
    c_pad = jnp.zeros((rows, d), F32).at[:b].set(c)
    tn = 1024
    out = pl.pallas_call(
        _ada_kernel,
        grid=(n_layers, n // tn),
        in_specs=[
            pl.BlockSpec((rows, d), lambda l, j: (0, 0)),
            pl.BlockSpec((None, d, tn), lambda l, j: (l, 0, j)),
            pl.BlockSpec((None, 1, tn), lambda l, j: (l, 0, j)),
        ],
        out_specs=pl.BlockSpec((None, rows, tn), lambda l, j: (l, 0, j)),
        out_shape=jax.ShapeDtypeStruct((n_layers, rows, n), F32),
        compiler_params=_params("parallel", "parallel"),
        name="ada_mod",
    )(c_pad, w_ada, b_ada.reshape(n_layers, 1, n))
    return out[:, :b].reshape(n_layers, b, N_MOD, 1, d)


def _norm_mod_kernel(x_ref, g_ref, sc_ref, sh_ref, o_ref):
    x = x_ref[...]
    y = x * lax.rsqrt(jnp.mean(x * x, axis=-1, keepdims=True) + EPS) * g_ref[...]
    o_ref[...] = (y * (1.0 + sc_ref[...]) + sh_ref[...]).astype(o_ref.dtype)


def _norm_mod(x2, g, mod, j_scale, j_shift, seq):
    t, d = x2.shape
    ts = min(512, seq)
    per_b = seq // ts
    return pl.pallas_call(
        _norm_mod_kernel,
        grid=(t // ts,),
        in_specs=[
            pl.BlockSpec((ts, d), lambda i: (i, 0)),
            pl.BlockSpec((1, d), lambda i: (0, 0)),
            pl.BlockSpec((None, None, 1, d), lambda i: (i // per_b, j_scale, 0, 0)),
            pl.BlockSpec((None, None, 1, d), lambda i: (i // per_b, j_shift, 0, 0)),
        ],
        out_specs=pl.BlockSpec((ts, d), lambda i: (i, 0)),
        out_shape=jax.ShapeDtypeStruct((t, d), BF16),
        compiler_params=_params("parallel"),
        name="norm_mod",
    )(x2, g.reshape(1, d), mod, mod)


def _mm_kernel(a_ref, w_ref, o_ref):
    o_ref[...] = _dot(a_ref[...], w_ref[...]).astype(o_ref.dtype)


def _matmul(a, w, out_dtype, tm, tn, name):
    t, k = a.shape
    n = w.shape[1]
    tm = min(tm, t)
    tn = min(tn, n)
    return pl.pallas_call(
        _mm_kernel,
        grid=(t // tm, n // tn),
        in_specs=[
            pl.BlockSpec((tm, k), lambda i, j: (i, 0)),
            pl.BlockSpec((k, tn), lambda i, j: (0, j)),
        ],
        out_specs=pl.BlockSpec((tm, tn), lambda i, j: (i, j)),
        out_shape=jax.ShapeDtypeStruct((t, n), out_dtype),
        compiler_params=_params("parallel", "parallel"),
        name=name,
    )(a, w)


def _mm_bias_kernel(a_ref, w_ref, b_ref, o_ref):
    o_ref[...] = _dot(a_ref[...], w_ref[...]) + b_ref[...]


def _matmul_bias_f32(a, w, bias, tm, name):
    t, k = a.shape
    n = w.shape[1]
    tm = min(tm, t)
    return pl.pallas_call(
        _mm_bias_kernel,
        grid=(t // tm,),
        in_specs=[
            pl.BlockSpec((tm, k), lambda i: (i, 0)),
            pl.BlockSpec((k, n), lambda i: (0, 0)),
            pl.BlockSpec((1, n), lambda i: (0, 0)),
        ],
        out_specs=pl.BlockSpec((tm, n), lambda i: (i, 0)),
        out_shape=jax.ShapeDtypeStruct((t, n), F32),
        compiler_params=_params("parallel"),
        name=name,
    )(a, w, bias)


def _mlstm_kernel(q_ref, k_ref, v_ref, o_ref, zif_ref, wconv_ref, g_ref, out_ref,
                  c_scr, n_scr, m_scr, qtail_scr, ktail_scr, *, dh):
    chunk = q_ref.shape[0]
    width = q_ref.shape[1]
    heads = width // dh

    @pl.when(pl.program_id(1) == 0)
    def _():
        c_scr[...] = jnp.zeros_like(c_scr)
        n_scr[...] = jnp.zeros_like(n_scr)
        m_scr[...] = jnp.zeros_like(m_scr)
        qtail_scr[...] = jnp.zeros_like(qtail_scr)
        ktail_scr[...] = jnp.zeros_like(ktail_scr)

    wconv = wconv_ref[...]

    def conv_silu(raw, tail_scr, w):
        xp = jnp.concatenate([tail_scr[...], raw], axis=0)
        acc = raw * w[M_CONV - 1:M_CONV]
        for back in range(1, M_CONV):
            shifted = pltpu.roll(xp, back, 0)[CONV_HALO:]
            acc = acc + shifted * w[M_CONV - 1 - back:M_CONV - back]
        tail_scr[...] = raw[chunk - CONV_HALO:]
        return _silu(acc)

    q_all = conv_silu(q_ref[...].astype(F32), qtail_scr, wconv[:, :width]) * (dh ** -0.5)
    k_all = conv_silu(k_ref[...].astype(F32), ktail_scr, wconv[:, width:])

    zif = zif_ref[...]
    lsf = _log_sigmoid(zif)
    zif_t = zif.T
    lsf_t = lsf.T

    row = lax.broadcasted_iota(jnp.int32, (chunk, chunk), 0)
    col = lax.broadcasted_iota(jnp.int32, (chunk, chunk), 1)
    causal = col <= row
    tri = causal.astype(F32)
    tri_t = (row <= col).astype(F32)

    for h in range(heads):
        sl = slice(h * dh, (h + 1) * dh)
        qh = q_all[:, sl]
        kh = k_all[:, sl]
        qb = qh.astype(BF16)
        kb = kh.astype(BF16)
        vb = v_ref[:, sl]

        i_col = zif[:, h:h + 1]
        i_row = zif_t[h:h + 1, :]
        lf_col = lsf[:, heads + h:heads + h + 1]
        lf_row = lsf_t[heads + h:heads + h + 1, :]

        b_colb = jnp.dot(tri, jnp.broadcast_to(lf_col, (chunk, chunk)),
                         preferred_element_type=F32, precision=HIGHEST)
        b_rowb = jnp.dot(jnp.broadcast_to(lf_row, (chunk, chunk)), tri_t,
                         preferred_element_type=F32, precision=HIGHEST)
        b_col = b_colb[:, :1]
        b_row = b_rowb[:1, :]
        b_last = b_row[:, chunk - 1:chunk]

        m_prev = m_scr[h][:, :1]
        dmat = jnp.where(causal, b_colb - b_rowb + i_row, NEG_INF)
        m_inter = b_col + m_prev
        m_t = jnp.maximum(jnp.max(dmat, axis=-1, keepdims=True), m_inter)
        w = _dot_nt(qb, kb) * jnp.exp(dmat - m_t)
        decay = jnp.exp(m_inter - m_t)
        c_prev = c_scr[h]
        n_prev = n_scr[h]
        num = _dot(w.astype(BF16), vb) + decay * _dot(qb, c_prev.astype(BF16))
        den = jnp.sum(w, axis=-1, keepdims=True) + decay * jnp.sum(qh * n_prev, axis=-1, keepdims=True)
        hh = num / jnp.maximum(jnp.abs(den), jnp.exp(-m_t))

        hn = hh * lax.rsqrt(jnp.mean(hh * hh, axis=-1, keepdims=True) + EPS) * g_ref[:, sl]
        out_ref[:, sl] = (hn * jax.nn.sigmoid(o_ref[:, sl].astype(F32))).astype(out_ref.dtype)

        g_col = b_last - b_col + i_col
        m_new = jnp.maximum(b_last + m_prev, jnp.max(g_col, axis=0, keepdims=True))
        wk = jnp.exp(g_col - m_new)
        carry_decay = jnp.exp(b_last + m_prev - m_new)
        kw = kh * wk
        c_scr[h] = carry_decay * c_prev + _dot_tn(kw.astype(BF16), vb)
        n_scr[h] = carry_decay * n_prev + jnp.sum(kw, axis=0, keepdims=True)
        m_scr[h] = jnp.broadcast_to(m_new, m_scr.shape[1:])


def _mlstm(zmain, zif, w_mconv, g_mout, batch, seq, width):
    t = zmain.shape[0]
    dh = width // M_HEADS
    nc = seq // M_CHUNK
    row_map = lambda cb: (lambda b, c: (b * nc + c, cb))
    return pl.pallas_call(
        functools.partial(_mlstm_kernel, dh=dh),
        grid=(batch, nc),
        in_specs=[
            pl.BlockSpec((M_CHUNK, width), row_map(0)),
            pl.BlockSpec((M_CHUNK, width), row_map(1)),
            pl.BlockSpec((M_CHUNK, width), row_map(2)),
            pl.BlockSpec((M_CHUNK, width), row_map(3)),
            pl.BlockSpec((M_CHUNK, LANE), row_map(0)),
            pl.BlockSpec((M_CONV, 2 * width), lambda b, c: (0, 0)),
            pl.BlockSpec((1, width), lambda b, c: (0, 0)),
        ],
        out_specs=pl.BlockSpec((M_CHUNK, width), row_map(0)),
        out_shape=jax.ShapeDtypeStruct((t, width), BF16),
        scratch_shapes=[
            pltpu.VMEM((M_HEADS, dh, dh), F32),
            pltpu.VMEM((M_HEADS, 1, dh), F32),
            pltpu.VMEM((M_HEADS, 1, LANE), F32),
            pltpu.VMEM((CONV_HALO, width), F32),
            pltpu.VMEM((CONV_HALO, width), F32),
        ],
        compiler_params=_params("parallel", "arbitrary"),
        name="mlstm",
    )(zmain, zmain, zmain, zmain, zif, w_mconv, g_mout.reshape(1, width))


def _qk_norm_kernel(q_ref, k_ref, gq_ref, gk_ref, qo_ref, ko_ref, *, dh):
    width = q_ref.shape[1]
    row = lax.broadcasted_iota(jnp.int32, (LANE, LANE), 0)
    col = lax.broadcasted_iota(jnp.int32, (LANE, LANE), 1)
    same_group = ((row // dh) == (col // dh)).astype(BF16)

    def group_rms(x_ref, g_ref, o_ref, scale):
        for tile in range(width // LANE):
            sl = slice(tile * LANE, (tile + 1) * LANE)
            x = x_ref[:, sl].astype(F32)
            sq = x * x
            hi = sq.astype(BF16)
            lo = (sq - hi.astype(F32)).astype(BF16)
            ssum = _dot(hi, same_group) + _dot(lo, same_group)
            y = x * lax.rsqrt(ssum * (1.0 / dh) + EPS) * g_ref[:, sl]
            o_ref[:, sl] = (y * scale).astype(o_ref.dtype)

    group_rms(q_ref, gq_ref, qo_ref, dh ** -0.5)
    group_rms(k_ref, gk_ref, ko_ref, 1.0)


def _qk_norm(zmain, g_dq, g_dk, width, col_q, col_k, seq):
    t = zmain.shape[0]
    dh = g_dq.shape[0]
    ts = min(512, seq)
    reps = width // dh
    return pl.pallas_call(
        functools.partial(_qk_norm_kernel, dh=dh),
        grid=(t // ts,),
        in_specs=[
            pl.BlockSpec((ts, width), lambda i: (i, col_q)),
            pl.BlockSpec((ts, width), lambda i: (i, col_k)),
            pl.BlockSpec((1, width), lambda i: (0, 0)),
            pl.BlockSpec((1, width), lambda i: (0, 0)),
        ],
        out_specs=[pl.BlockSpec((ts, width), lambda i: (i, 0)),
                   pl.BlockSpec((ts, width), lambda i: (i, 0))],
        out_shape=[jax.ShapeDtypeStruct((t, width), BF16), jax.ShapeDtypeStruct((t, width), BF16)],
        compiler_params=_params("parallel"),
        name="qk_norm",
    )(zmain, zmain, jnp.tile(g_dq, reps).reshape(1, width), jnp.tile(g_dk, reps).reshape(1, width))


def _diff_attn_kernel(q_ref, k_ref, v_ref, lq1_ref, lk1_ref, lq2_ref, lk2_ref, g_ref, out_ref,
                      m_scr, l_scr, acc_scr, *, lam_init, dh):
    tq = q_ref.shape[0]
    i = pl.program_id(2)
    q = q_ref[...]
    lane = lax.broadcasted_iota(jnp.int32, q.shape, 1)
    zero = jnp.zeros_like(q)
    q_halves = (jnp.where(lane < dh, q, zero), jnp.where(lane >= dh, q, zero))

    m_scr[...] = jnp.full_like(m_scr, NEG_INF)
    l_scr[...] = jnp.zeros_like(l_scr)
    acc_scr[...] = jnp.zeros_like(acc_scr)

    def block(j, masked):
        start = pl.multiple_of(j * tq, tq)
        k = k_ref[pl.ds(start, tq), :]
        v = v_ref[pl.ds(start, tq), :]
        for c in range(2):
            s = _dot_nt(k, q_halves[c])
            if masked:
                key = lax.broadcasted_iota(jnp.int32, s.shape, 0)
                qry = lax.broadcasted_iota(jnp.int32, s.shape, 1)
                s = jnp.where(key <= qry, s, NEG_INF)
            m_prev = m_scr[c]
            m_new = jnp.maximum(m_prev, jnp.max(s, axis=0, keepdims=True))
            alpha = jnp.exp(m_prev - m_new)
            p = jnp.exp(s - m_new)
            l_scr[c] = alpha * l_scr[c] + jnp.sum(p, axis=0, keepdims=True)
            acc_scr[c] = alpha * acc_scr[c] + _dot_tn(v, p.astype(BF16))
            m_scr[c] = m_new

    def body(j, carry):
        block(j, False)
        return carry

    lax.fori_loop(0, i, body, 0)
    block(i, True)

    lam = (jnp.exp(jnp.sum(lq1_ref[...] * lk1_ref[...], axis=-1, keepdims=True))
           - jnp.exp(jnp.sum(lq2_ref[...] * lk2_ref[...], axis=-1, keepdims=True)) + lam_init)
    o = (acc_scr[0] / l_scr[0] - lam * (acc_scr[1] / l_scr[1])).T
    on = o * lax.rsqrt(jnp.mean(o * o, axis=-1, keepdims=True) + EPS) * g_ref[...]
    out_ref[...] = (on * (1.0 - lam_init)).astype(out_ref.dtype)


def _diff_attn(qn, kn, zmain, col_v, lam_params, g_dsub, lam_init, batch, seq, width):
    t = qn.shape[0]
    dh = lam_params[0].shape[0]
    hw = 2 * dh
    heads = width // hw
    tq = min(512, seq)
    nq = seq // tq
    v_col0 = col_v * (width // hw)
    lam_specs = [pl.BlockSpec((1, dh), lambda b, h, i: (0, 0)) for _ in range(4)]
    return pl.pallas_call(
        functools.partial(_diff_attn_kernel, lam_init=lam_init, dh=dh),
        grid=(batch, heads, nq),
        in_specs=[
            pl.BlockSpec((tq, hw), lambda b, h, i: (b * nq + i, h)),
            pl.BlockSpec((seq, hw), lambda b, h, i: (b, h)),
            pl.BlockSpec((seq, hw), lambda b, h, i: (b, v_col0 + h)),
            *lam_specs,
            pl.BlockSpec((1, hw), lambda b, h, i: (0, 0)),
        ],
        out_specs=pl.BlockSpec((tq, hw), lambda b, h, i: (b * nq + i, h)),
        out_shape=jax.ShapeDtypeStruct((t, width), BF16),
        scratch_shapes=[
            pltpu.VMEM((2, 1, tq), F32),
            pltpu.VMEM((2, 1, tq), F32),
            pltpu.VMEM((2, hw, tq), F32),
        ],
        compiler_params=_params("parallel", "parallel", "arbitrary"),
        name="diff_attn",
    )(qn, kn, zmain, *[p.reshape(1, dh) for p in lam_params], g_dsub.reshape(1, hw))


def _stick_kernel(q_ref, k_ref, v_ref, out_ref, carry_scr, acc_scr, *, sub):
    tq, dh = q_ref.shape
    i = pl.program_id(2)
    q = q_ref[...]
    n_sub = tq // sub
    sign_bit = jnp.uint32(1 << 31)

    srow = lax.broadcasted_iota(jnp.int32, (sub, 2 * sub), 0)
    jcol = lax.broadcasted_iota(jnp.int32, (sub, 2 * sub), 1)
    later = ((jcol & (sub - 1)) > srow).astype(BF16)

    carry_scr[...] = jnp.zeros_like(carry_scr)
    acc_scr[...] = jnp.zeros_like(acc_scr)

    def block(jb, masked):
        start = pl.multiple_of(jb * tq, tq)
        k = k_ref[pl.ds(start, tq), :]
        v = v_ref[pl.ds(start, tq), :]
        z2 = _dot_nt(k, q)
        neg_abs = pltpu.bitcast(pltpu.bitcast(z2, jnp.uint32) | sign_bit, F32)
        sp2 = jnp.maximum(z2, 0.0) + jnp.log(1.0 + jnp.exp2(neg_abs)) * LOG2E
        if masked:
            key = lax.broadcasted_iota(jnp.int32, z2.shape, 0)
            qry = lax.broadcasted_iota(jnp.int32, z2.shape, 1)
            mask = key < qry
            sp2 = jnp.where(mask, sp2, 0.0)
        hi = sp2.astype(BF16)
        lo = (sp2 - hi.astype(F32)).astype(BF16)
        carry = carry_scr[...]
        parts = [None] * n_sub
        for c in reversed(range(n_sub)):
            rows = slice(c * sub, (c + 1) * sub)
            suffix = _dot(later, jnp.concatenate([hi[rows], lo[rows]], axis=0)) + carry
            a = jnp.exp2(z2[rows] - sp2[rows] - suffix)
            if masked:
                a = jnp.where(mask[rows], a, 0.0)
            parts[c] = a.astype(BF16)
            carry = carry + jnp.sum(sp2[rows], axis=0, keepdims=True)
        acc_scr[...] += _dot_tn(v, jnp.concatenate(parts, axis=0))
        carry_scr[...] = carry
        return jnp.min(carry)

    def keep_going(state):
        jb, min_carry = state
        return jnp.logical_and(jb >= 0, min_carry <= STICK_DEAD_LOG2)

    def body(state):
        jb, _ = state
        return jb - 1, block(jb, False)

    lax.while_loop(keep_going, body, (i - 1, block(i, True)))
    out_ref[...] = acc_scr[...].T.astype(out_ref.dtype)


def _stick_breaking(zmain, col_q, col_k, col_v, batch, seq, width):
    t = zmain.shape[0]
    dh = width // SB_HEADS
    tq = min(512, seq)
    nq = seq // tq
    per = width // dh
    return pl.pallas_call(
        functools.partial(_stick_kernel, sub=LANE),
        grid=(batch, SB_HEADS, nq),
        in_specs=[
            pl.BlockSpec((tq, dh), lambda b, h, i: (b * nq + i, col_q * per + h)),
            pl.BlockSpec((seq, dh), lambda b, h, i: (b, col_k * per + h)),
            pl.BlockSpec((seq, dh), lambda b, h, i: (b, col_v * per + h)),
        ],
        out_specs=pl.BlockSpec((tq, dh), lambda b, h, i: (b * nq + i, h)),
        out_shape=jax.ShapeDtypeStruct((t, width), BF16),
        scratch_shapes=[
            pltpu.VMEM((1, tq), F32),
            pltpu.VMEM((dh, tq), F32),
        ],
        compiler_params=_params("parallel", "parallel", "arbitrary"),
        name="stick_breaking",
    )(zmain, zmain, zmain)


def _merge_kernel(om_ref, od_ref, os_ref, wm_ref, wd_ref, ws_ref, gm_ref, gd_ref, gs_ref, o_ref):
    acc = jax.nn.sigmoid(gm_ref[...].astype(F32)) * _dot(om_ref[...], wm_ref[...])
    acc = acc + jax.nn.sigmoid(gd_ref[...].astype(F32)) * _dot(od_ref[...], wd_ref[...])
    acc = acc + jax.nn.sigmoid(gs_ref[...].astype(F32)) * _dot(os_ref[...], ws_ref[...])
    o_ref[...] = acc.astype(o_ref.dtype)


def _merge(out_m, out_d, out_s, w_branch, zmain, gate_col0, d_model):
    t, width = out_m.shape
    tm = min(1024, t)
    tn = 1024
    nj = d_model // tn
    act_spec = pl.BlockSpec((tm, width), lambda i, j: (i, 0))
    w_spec = lambda br: pl.BlockSpec((None, width, tn), lambda i, j: (br, 0, j))
    gate_spec = lambda br: pl.BlockSpec((tm, tn), lambda i, j: (i, gate_col0 + br * nj + j))
    return pl.pallas_call(
        _merge_kernel,
        grid=(t // tm, nj),
        in_specs=[act_spec, act_spec, act_spec, w_spec(0), w_spec(1), w_spec(2),
                  gate_spec(0), gate_spec(1), gate_spec(2)],
        out_specs=pl.BlockSpec((tm, tn), lambda i, j: (i, j)),
        out_shape=jax.ShapeDtypeStruct((t, d_model), BF16),
        compiler_params=_params("parallel", "parallel"),
        name="branch_merge",
    )(out_m, out_d, out_s, w_branch, w_branch, w_branch, zmain, zmain, zmain)


def _proj_residual_kernel(a_ref, w_ref, x_ref, gt_ref, o_ref):
    o_ref[...] = x_ref[...] + gt_ref[...] * _dot(a_ref[...], w_ref[...])


def _proj_residual(a, w, x2, mod, j_gate, seq, tm, tn, name):
    t, k = a.shape
    n = w.shape[1]
    tm = min(tm, seq)
    per_b = seq // tm
    return pl.pallas_call(
        _proj_residual_kernel,
        grid=(t // tm, n // tn),
        in_specs=[
            pl.BlockSpec((tm, k), lambda i, j: (i, 0)),
            pl.BlockSpec((k, tn), lambda i, j: (0, j)),
            pl.BlockSpec((tm, tn), lambda i, j: (i, j)),
            pl.BlockSpec((None, None, 1, tn), lambda i, j: (i // per_b, j_gate, 0, j)),
        ],
        out_specs=pl.BlockSpec((tm, tn), lambda i, j: (i, j)),
        out_shape=jax.ShapeDtypeStruct((t, n), F32),
        compiler_params=_params("parallel", "parallel"),
        name=name,
    )(a, w, x2, mod)


def _ffn_up_kernel(h_ref, wg_ref, wu_ref, wc_ref, o_ref, tail_scr, *, tiles_per_seq):
    tm = h_ref.shape[0]

    @pl.when(pl.program_id(1) % tiles_per_seq == 0)
    def _():
        tail_scr[...] = jnp.zeros_like(tail_scr)

    h = h_ref[...]
    gate = _dot(h, wg_ref[...])
    up = _dot(h, wu_ref[...])
    wc = wc_ref[...]
    xp = jnp.concatenate([tail_scr[...], gate], axis=0)
    conv = gate * wc[FF_CONV - 1:FF_CONV]
    for back in range(1, FF_CONV):
        conv = conv + pltpu.roll(xp, back, 0)[CONV_HALO:] * wc[FF_CONV - 1 - back:FF_CONV - back]
    tail_scr[...] = gate[tm - CONV_HALO:]
    o_ref[...] = (_silu(conv) * up).astype(o_ref.dtype)


def _ffn_up(h2, w_up, w_ffconv, seq):
    t, k = h2.shape
    d_ff = w_ffconv.shape[1]
    tm = min(1024, seq)
    tn = 512
    nj = d_ff // tn
    return pl.pallas_call(
        functools.partial(_ffn_up_kernel, tiles_per_seq=seq // tm),
        grid=(nj, t // tm),
        in_specs=[
            pl.BlockSpec((tm, k), lambda j, i: (i, 0)),
            pl.BlockSpec((k, tn), lambda j, i: (0, j)),
            pl.BlockSpec((k, tn), lambda j, i: (0, nj + j)),
            pl.BlockSpec((FF_CONV, tn), lambda j, i: (0, j)),
        ],
        out_specs=pl.BlockSpec((tm, tn), lambda j, i: (i, j)),
        out_shape=jax.ShapeDtypeStruct((t, d_ff), BF16),
        scratch_shapes=[pltpu.VMEM((CONV_HALO, tn), F32)],
        compiler_params=_params("parallel", "arbitrary"),
        name="ffn_up",
    )(h2, w_up, w_up, w_ffconv)


def kernel(x, c, w_ada, b_ada, g_mix, g_ffn, w_in, b_gate_if, w_mconv, g_mout, g_dq, g_dk,
           lam_q1, lam_k1, lam_q2, lam_k2, g_dsub, w_branch, w_out, w_up, w_ffconv, w_down):
    batch, seq, d = x.shape
    depth = w_ada.shape[0]
    width = d // 2
    t = batch * seq
    n_gate = 2 * M_HEADS

    mod_all = _ada_mod(c, w_ada, b_ada)
    x2 = x.reshape(t, d)

    for l in range(depth):
        mod = mod_all[l]
        w_l = w_in[l]
        gate0 = 4 * width
        sq0 = gate0 + n_gate + 3 * width
        sb_scale = (width // SB_HEADS) ** -0.5 * LOG2E
        w_main = jnp.concatenate([w_l[:, :gate0], w_l[:, gate0 + n_gate:sq0],
                                  w_l[:, sq0:sq0 + width] * sb_scale, w_l[:, sq0 + width:]],
                                 axis=1).astype(BF16)
        w_if = jnp.zeros((d, LANE), BF16).at[:, :n_gate].set(w_l[:, gate0:gate0 + n_gate].astype(BF16))
        b_if = jnp.zeros((1, LANE), F32).at[0, :n_gate].set(b_gate_if[l])

        h = _norm_mod(x2, g_mix[l], mod, 1, 0, seq)
        zmain = _matmul(h, w_main, BF16, 1024, 1024, "in_proj")
        zif = _matmul_bias_f32(h, w_if, b_if, 1024, "gate_proj")

        out_m = _mlstm(zmain, zif, w_mconv[l], g_mout[l], batch, seq, width)
        qn, kn = _qk_norm(zmain, g_dq[l], g_dk[l], width, 4, 5, seq)
        lam_init = 0.8 - 0.6 * math.exp(-0.3 * l)
        out_d = _diff_attn(qn, kn, zmain, 6, (lam_q1[l], lam_k1[l], lam_q2[l], lam_k2[l]),
                           g_dsub[l], lam_init, batch, seq, width)
        out_s = _stick_breaking(zmain, 7, 8, 9, batch, seq, width)

        merged = _merge(out_m, out_d, out_s, w_branch[l].astype(BF16), zmain, 10 * width // 1024, d)
        x2 = _proj_residual(merged, w_out[l].astype(BF16), x2, mod, 2, seq, 1024, 1024, "out_proj")

        h2 = _norm_mod(x2, g_ffn[l], mod, 4, 3, seq)
        act = _ffn_up(h2, w_up[l].astype(BF16), w_ffconv[l], seq)
        x2 = _proj_residual(act, w_down[l].astype(BF16), x2, mod, 5, seq, 512, 512, "ffn_down")

    return x2.reshape(batch, seq, d)
```

```python
import functools
import math

import jax
import jax.numpy as jnp
from jax import lax
from jax.experimental import pallas as pl
from jax.experimental.pallas import tpu as pltpu

EPS = 1e-6
M_HEADS = 4
M_CONV = 4
M_CHUNK = 128
DA_HEADS = 8
SB_HEADS = 8
N_MOD = 6
FF_CONV = 3
LANE = 128
CONV_HALO = 8
VMEM_LIMIT_BYTES = 56 * 1024 * 1024
HIGHEST = lax.Precision.HIGHEST
NEG_INF = float("-inf")
LOG2E = math.log2(math.e)
STICK_DEAD_LOG2 = 200.0

BF16 = jnp.bfloat16
F32 = jnp.float32


def _params(*sem):
    return pltpu.CompilerParams(dimension_semantics=sem, vmem_limit_bytes=VMEM_LIMIT_BYTES)


def _dot(a, b):
    return jnp.dot(a, b, preferred_element_type=F32)


def _dot_nt(a, b):
    return lax.dot_general(a, b, (((1,), (1,)), ((), ())), preferred_element_type=F32)


def _dot_tn(a, b):
    return lax.dot_general(a, b, (((0,), (0,)), ((), ())), preferred_element_type=F32)


def _silu(x):
    return x * jax.nn.sigmoid(x)


def _softplus(x):
    return jnp.maximum(x, 0.0) + jnp.log1p(jnp.exp(-jnp.abs(x)))


def _log_sigmoid(x):
    return -_softplus(-x)


def _ada_kernel(c_ref, w_ref, b_ref, o_ref):
    c = c_ref[...]
    o_ref[...] = jnp.dot(_silu(c), w_ref[...], preferred_element_type=F32,
                         precision=HIGHEST) + b_ref[...]


def _ada_mod(c, w_ada, b_ada):
    n_layers, d, n = w_ada.shape
    b = c.shape[0]
    rows = 8
    c_pad = jnp.zeros((rows, d), F32).at[:b].set(c)
    tn = 1024
    out = pl.pallas_call(
        _ada_kernel,
        grid=(n_layers, n // tn),
        in_specs=[
            pl.BlockSpec((rows, d), lambda l, j: (0, 0)),
            pl.BlockSpec((None, d, tn), lambda l, j: (l, 0, j)),
            pl.BlockSpec((None, 1, tn), lambda l, j: (l, 0, j)),
        ],
        out_specs=pl.BlockSpec((None, rows, tn), lambda l, j: (l, 0, j)),
        out_shape=jax.ShapeDtypeStruct((n_layers, rows, n), F32),
        compiler_params=_params("parallel", "parallel"),
        name="ada_mod",
    )(c_pad, w_ada, b_ada.reshape(n_layers, 1, n))
    return out[:, :b].reshape(n_layers, b, N_MOD, 1, d)


def _norm_mod_kernel(x_ref, g_ref, sc_ref, sh_ref, o_ref):
    x = x_ref[...]
    y = x * lax.rsqrt(jnp.mean(x * x, axis=-1, keepdims=True) + EPS) * g_ref[...]
    o_ref[...] = (y * (1.0 + sc_ref[...]) + sh_ref[...]).astype(o_ref.dtype)


def _norm_mod(x2, g, mod, j_scale, j_shift, seq):
    t, d = x2.shape
    ts = min(512, seq)
    per_b = seq // ts
    return pl.pallas_call(
        _norm_mod_kernel,
        grid=(t // ts,),
        in_specs=[
            pl.BlockSpec((ts, d), lambda i: (i, 0)),
            pl.BlockSpec((1, d), lambda i: (0, 0)),
            pl.BlockSpec((None, None, 1, d), lambda i: (i // per_b, j_scale, 0, 0)),
            pl.BlockSpec((None, None, 1, d), lambda i: (i // per_b, j_shift, 0, 0)),
        ],
        out_specs=pl.BlockSpec((ts, d), lambda i: (i, 0)),
        out_shape=jax.ShapeDtypeStruct((t, d), BF16),
        compiler_params=_params("parallel"),
        name="norm_mod",
    )(x2, g.reshape(1, d), mod, mod)


def _mm_kernel(a_ref, w_ref, o_ref):
    o_ref[...] = _dot(a_ref[...], w_ref[...]).astype(o_ref.dtype)


def _matmul(a, w, out_dtype, tm, tn, name):
    t, k = a.shape
    n = w.shape[1]
    tm = min(tm, t)
    tn = min(tn, n)
    return pl.pallas_call(
        _mm_kernel,
        grid=(t // tm, n // tn),
        in_specs=[
            pl.BlockSpec((tm, k), lambda i, j: (i, 0)),
            pl.BlockSpec((k, tn), lambda i, j: (0, j)),
        ],
        out_specs=pl.BlockSpec((tm, tn), lambda i, j: (i, j)),
        out_shape=jax.ShapeDtypeStruct((t, n), out_dtype),
        compiler_params=_params("parallel", "parallel"),
        name=name,
    )(a, w)


def _regroup_kernel(a_ref, b_ref, o_ref, *, aligned_blocks, shift, scaled_block, scale):
    m = pl.program_id(1)
    tn = o_ref.shape[1]

    @pl.when(m < aligned_blocks)
    def _():
        o_ref[...] = a_ref[...].astype(o_ref.dtype)

    @pl.when(m >= aligned_blocks)
    def _():
        both = jnp.concatenate([a_ref[...], b_ref[...]], axis=1)
        w = both[:, shift:shift + tn] * jnp.where(m == scaled_block, scale, 1.0)
        o_ref[...] = w.astype(o_ref.dtype)


def _regroup_in_weights(w_in, layer, width, n_gate, scaled_block, scale):
    _, d, n_in = w_in.shape
    tn = width
    n_out = n_in - n_gate
    aligned_blocks = 4
    tr = 256
    last = pl.cdiv(n_in, tn) - 1
    return pl.pallas_call(
        functools.partial(_regroup_kernel, aligned_blocks=aligned_blocks, shift=n_gate,
                          scaled_block=scaled_block, scale=scale),
        grid=(d // tr, n_out // tn),
        in_specs=[
            pl.BlockSpec((None, tr, tn), lambda r, m: (layer, r, m)),
            pl.BlockSpec((None, tr, tn), lambda r, m: (layer, r, jnp.minimum(m + 1, last))),
        ],
        out_specs=pl.BlockSpec((tr, tn), lambda r, m: (r, m)),
        out_shape=jax.ShapeDtypeStruct((d, n_out), BF16),
        compiler_params=_params("parallel", "parallel"),
        name="regroup_w_in",
    )(w_in, w_in)


def _gate_proj_kernel(a_ref, w_ref, b_ref, o_ref, *, n_gate):
    w = w_ref[...]
    lane = lax.broadcasted_iota(jnp.int32, w.shape, 1)
    w = jnp.where(lane < n_gate, w, 0.0).astype(BF16)
    o_ref[...] = _dot(a_ref[...], w) + b_ref[...]


def _gate_proj(a, w_in, layer, gate0, bias, n_gate):
    t, k = a.shape
    tm = min(1024, t)
    return pl.pallas_call(
        functools.partial(_gate_proj_kernel, n_gate=n_gate),
        grid=(t // tm,),
        in_specs=[
            pl.BlockSpec((tm, k), lambda i: (i, 0)),
            pl.BlockSpec((None, k, LANE), lambda i: (layer, 0, gate0 // LANE)),
            pl.BlockSpec((1, LANE), lambda i: (0, 0)),
        ],
        out_specs=pl.BlockSpec((tm, LANE), lambda i: (i, 0)),
        out_shape=jax.ShapeDtypeStruct((t, LANE), F32),
        compiler_params=_params("parallel"),
        name="gate_proj",
    )(a, w_in, bias)


def _mlstm_kernel(q_ref, k_ref, v_ref, o_ref, zif_ref, wconv_ref, g_ref, out_ref,
                  c_scr, n_scr, m_scr, qtail_scr, ktail_scr, *, dh):
    chunk = q_ref.shape[0]
    width = q_ref.shape[1]
    heads = width // dh

    @pl.when(pl.program_id(1) == 0)
    def _():
        c_scr[...] = jnp.zeros_like(c_scr)
        n_scr[...] = jnp.zeros_like(n_scr)
        m_scr[...] = jnp.zeros_like(m_scr)
        qtail_scr[...] = jnp.zeros_like(qtail_scr)
        ktail_scr[...] = jnp.zeros_like(ktail_scr)

    wconv = wconv_ref[...]

    def conv_silu(raw, tail_scr, w):
        xp = jnp.concatenate([tail_scr[...], raw], axis=0)
        acc = raw * w[M_CONV - 1:M_CONV]
        for back in range(1, M_CONV):
            shifted = pltpu.roll(xp, back, 0)[CONV_HALO:]
            acc = acc + shifted * w[M_CONV - 1 - back:M_CONV - back]
        tail_scr[...] = raw[chunk - CONV_HALO:]
        return _silu(acc)

    q_all = conv_silu(q_ref[...].astype(F32), qtail_scr, wconv[:, :width]) * (dh ** -0.5)
    k_all = conv_silu(k_ref[...].astype(F32), ktail_scr, wconv[:, width:])

    zif = zif_ref[...]
    lsf = _log_sigmoid(zif)
    zif_t = zif.T
    lsf_t = lsf.T

    row = lax.broadcasted_iota(jnp.int32, (chunk, chunk), 0)
    col = lax.broadcasted_iota(jnp.int32, (chunk, chunk), 1)
    causal = col <= row
    tri = causal.astype(F32)
    tri_t = (row <= col).astype(F32)

    for h in range(heads):
        sl = slice(h * dh, (h + 1) * dh)
        qh = q_all[:, sl]
        kh = k_all[:, sl]
        qb = qh.astype(BF16)
        kb = kh.astype(BF16)
        vb = v_ref[:, sl]

        i_col = zif[:, h:h + 1]
        i_row = zif_t[h:h + 1, :]
        lf_col = lsf[:, heads + h:heads + h + 1]
        lf_row = lsf_t[heads + h:heads + h + 1, :]

        b_colb = jnp.dot(tri, jnp.broadcast_to(lf_col, (chunk, chunk)),
                         preferred_element_type=F32, precision=HIGHEST)
        b_rowb = jnp.dot(jnp.broadcast_to(lf_row, (chunk, chunk)), tri_t,
                         preferred_element_type=F32, precision=HIGHEST)
        b_col = b_colb[:, :1]
        b_row = b_rowb[:1, :]
        b_last = b_row[:, chunk - 1:chunk]

        m_prev = m_scr[h][:, :1]
        dmat = jnp.where(causal, b_colb - b_rowb + i_row, NEG_INF)
        m_inter = b_col + m_prev
        m_t = jnp.maximum(jnp.max(dmat, axis=-1, keepdims=True), m_inter)
        w = _dot_nt(qb, kb) * jnp.exp(dmat - m_t)
        decay = jnp.exp(m_inter - m_t)
        c_prev = c_scr[h]
        n_prev = n_scr[h]
        num = _dot(w.astype(BF16), vb) + decay * _dot(qb, c_prev.astype(BF16))
        den = jnp.sum(w, axis=-1, keepdims=True) + decay * jnp.sum(qh * n_prev, axis=-1, keepdims=True)
        hh = num / jnp.maximum(jnp.abs(den), jnp.exp(-m_t))

        hn = hh * lax.rsqrt(jnp.mean(hh * hh, axis=-1, keepdims=True) + EPS) * g_ref[:, sl]
        out_ref[:, sl] = (hn * jax.nn.sigmoid(o_ref[:, sl].astype(F32))).astype(out_ref.dtype)

        g_col = b_last - b_col + i_col
        m_new = jnp.maximum(b_last + m_prev, jnp.max(g_col, axis=0, keepdims=True))
        wk = jnp.exp(g_col - m_new)
        carry_decay = jnp.exp(b_last + m_prev - m_new)
        kw = kh * wk
        c_scr[h] = carry_decay * c_prev + _dot_tn(kw.astype(BF16), vb)
        n_scr[h] = carry_decay * n_prev + jnp.sum(kw, axis=0, keepdims=True)
        m_scr[h] = jnp.broadcast_to(m_new, m_scr.shape[1:])


def _mlstm(zmain, zif, w_mconv, g_mout, batch, seq, width):
    t = zmain.shape[0]
    dh = width // M_HEADS
    nc = seq // M_CHUNK
    row_map = lambda cb: (lambda b, c: (b * nc + c, cb))
    return pl.pallas_call(
        functools.partial(_mlstm_kernel, dh=dh),
        grid=(batch, nc),
        in_specs=[
            pl.BlockSpec((M_CHUNK, width), row_map(0)),
            pl.BlockSpec((M_CHUNK, width), row_map(1)),
            pl.BlockSpec((M_CHUNK, width), row_map(2)),
            pl.BlockSpec((M_CHUNK, width), row_map(3)),
            pl.BlockSpec((M_CHUNK, LANE), row_map(0)),
            pl.BlockSpec((M_CONV, 2 * width), lambda b, c: (0, 0)),
            pl.BlockSpec((1, width), lambda b, c: (0, 0)),
        ],
        out_specs=pl.BlockSpec((M_CHUNK, width), row_map(0)),
        out_shape=jax.ShapeDtypeStruct((t, width), BF16),
        scratch_shapes=[
            pltpu.VMEM((M_HEADS, dh, dh), F32),
            pltpu.VMEM((M_HEADS, 1, dh), F32),
            pltpu.VMEM((M_HEADS, 1, LANE), F32),
            pltpu.VMEM((CONV_HALO, width), F32),
            pltpu.VMEM((CONV_HALO, width), F32),
        ],
        compiler_params=_params("parallel", "arbitrary"),
        name="mlstm",
    )(zmain, zmain, zmain, zmain, zif, w_mconv, g_mout.reshape(1, width))


def _qk_norm_kernel(q_ref, k_ref, gq_ref, gk_ref, qo_ref, ko_ref, *, dh):
    width = q_ref.shape[1]
    row = lax.broadcasted_iota(jnp.int32, (LANE, LANE), 0)
    col = lax.broadcasted_iota(jnp.int32, (LANE, LANE), 1)
    same_group = ((row // dh) == (col // dh)).astype(BF16)

    def group_rms(x_ref, g_ref, o_ref, scale):
        for tile in range(width // LANE):
            sl = slice(tile * LANE, (tile + 1) * LANE)
            x = x_ref[:, sl].astype(F32)
            sq = x * x
            hi = sq.astype(BF16)
            lo = (sq - hi.astype(F32)).astype(BF16)
            ssum = _dot(hi, same_group) + _dot(lo, same_group)
            y = x * lax.rsqrt(ssum * (1.0 / dh) + EPS) * g_ref[:, sl]
            o_ref[:, sl] = (y * scale).astype(o_ref.dtype)

    group_rms(q_ref, gq_ref, qo_ref, dh ** -0.5 * LOG2E)
    group_rms(k_ref, gk_ref, ko_ref, 1.0)


def _qk_norm(zmain, g_dq, g_dk, width, col_q, col_k, seq):
    t = zmain.shape[0]
    dh = g_dq.shape[0]
    ts = min(512, seq)
    reps = width // dh
    return pl.pallas_call(
        functools.partial(_qk_norm_kernel, dh=dh),
        grid=(t // ts,),
        in_specs=[
            pl.BlockSpec((ts, width), lambda i: (i, col_q)),
            pl.BlockSpec((ts, width), lambda i: (i, col_k)),
            pl.BlockSpec((1, width), lambda i: (0, 0)),
            pl.BlockSpec((1, width), lambda i: (0, 0)),
        ],
        out_specs=[pl.BlockSpec((ts, width), lambda i: (i, 0)),
                   pl.BlockSpec((ts, width), lambda i: (i, 0))],
        out_shape=[jax.ShapeDtypeStruct((t, width), BF16), jax.ShapeDtypeStruct((t, width), BF16)],
        compiler_params=_params("parallel"),
        name="qk_norm",
    )(zmain, zmain, jnp.tile(g_dq, reps).reshape(1, width), jnp.tile(g_dk, reps).reshape(1, width))


def _diff_attn_kernel(q_ref, k_ref, v_ref, lq1_ref, lk1_ref, lq2_ref, lk2_ref, g_ref, out_ref,
                      m_scr, l_scr, acc_scr, s_scr, *, lam_init, dh):
    tq = q_ref.shape[0]
    i = pl.program_id(2)
    q = q_ref[...]
    lane = lax.broadcasted_iota(jnp.int32, q.shape, 1)
    zero = jnp.zeros_like(q)
    q_halves = (jnp.where(lane < dh, q, zero), jnp.where(lane >= dh, q, zero))

    m_scr[...] = jnp.full_like(m_scr, NEG_INF)
    l_scr[...] = jnp.zeros_like(l_scr)
    acc_scr[...] = jnp.zeros_like(acc_scr)

    def key_block(ref, j):
        return ref[pl.ds(pl.multiple_of(j * tq, tq), tq), :]

    def put_scores(j, slot):
        k = key_block(k_ref, j)
        for c in range(2):
            s_scr[slot, c] = _dot_nt(k, q_halves[c])

    def consume(j, slot, masked):
        v = key_block(v_ref, j)
        probs, alphas = [], []
        for c in range(2):
            s = s_scr[slot, c]
            if masked:
                key = lax.broadcasted_iota(jnp.int32, s.shape, 0)
                qry = lax.broadcasted_iota(jnp.int32, s.shape, 1)
                s = jnp.where(key <= qry, s, NEG_INF)
            m_prev = m_scr[c]
            m_new = jnp.maximum(m_prev, jnp.max(s, axis=0, keepdims=True))
            alpha = jnp.exp2(m_prev - m_new)
            p = jnp.exp2(s - m_new)
            l_scr[c] = alpha * l_scr[c] + jnp.sum(p, axis=0, keepdims=True)
            m_scr[c] = m_new
            probs.append(p.astype(BF16))
            alphas.append(alpha)
        for c in range(2):
            acc_scr[c] = alphas[c] * acc_scr[c] + _dot_tn(v, probs[c])

    put_scores(0, 0)

    def pair(p, carry):
        j = 2 * p
        put_scores(j + 1, 1)
        consume(j, 0, False)
        put_scores(j + 2, 0)
        consume(j + 1, 1, False)
        return carry

    lax.fori_loop(0, i // 2, pair, 0)

    @pl.when(i % 2 == 0)
    def _():
        consume(i, 0, True)

    @pl.when(i % 2 == 1)
    def _():
        put_scores(i, 1)
        consume(i - 1, 0, False)
        consume(i, 1, True)

    lam = (jnp.exp(jnp.sum(lq1_ref[...] * lk1_ref[...], axis=-1, keepdims=True))
           - jnp.exp(jnp.sum(lq2_ref[...] * lk2_ref[...], axis=-1, keepdims=True)) + lam_init)
    o = (acc_scr[0] / l_scr[0] - lam * (acc_scr[1] / l_scr[1])).T
    on = o * lax.rsqrt(jnp.mean(o * o, axis=-1, keepdims=True) + EPS) * g_ref[...]
    out_ref[...] = (on * (1.0 - lam_init)).astype(out_ref.dtype)


def _diff_attn(qn, kn, zmain, col_v, lam_params, g_dsub, lam_init, batch, seq, width):
    t = qn.shape[0]
    dh = lam_params[0].shape[0]
    hw = 2 * dh
    heads = width // hw
    tq = min(512, seq)
    nq = seq // tq
    v_col0 = col_v * (width // hw)
    lam_specs = [pl.BlockSpec((1, dh), lambda b, h, i: (0, 0)) for _ in range(4)]
    return pl.pallas_call(
        functools.partial(_diff_attn_kernel, lam_init=lam_init, dh=dh),
        grid=(batch, heads, nq),
        in_specs=[
            pl.BlockSpec((tq, hw), lambda b, h, i: (b * nq + i, h)),
            pl.BlockSpec((seq, hw), lambda b, h, i: (b, h)),
            pl.BlockSpec((seq, hw), lambda b, h, i: (b, v_col0 + h)),
            *lam_specs,
            pl.BlockSpec((1, hw), lambda b, h, i: (0, 0)),
        ],
        out_specs=pl.BlockSpec((tq, hw), lambda b, h, i: (b * nq + i, h)),
        out_shape=jax.ShapeDtypeStruct((t, width), BF16),
        scratch_shapes=[
            pltpu.VMEM((2, 1, tq), F32),
            pltpu.VMEM((2, 1, tq), F32),
            pltpu.VMEM((2, hw, tq), F32),
            pltpu.VMEM((2, 2, tq, tq), F32),
        ],
        compiler_params=_params("parallel", "parallel", "arbitrary"),
        name="diff_attn",
    )(qn, kn, zmain, *[p.reshape(1, dh) for p in lam_params], g_dsub.reshape(1, hw))


def _stick_kernel(q_ref, k_ref, v_ref, out_ref, carry_scr, acc_scr, *, sub):
    tq, dh = q_ref.shape
    i = pl.program_id(2)
    q = q_ref[...]
    n_sub = tq // sub
    sign_bit = jnp.uint32(1 << 31)

    srow = lax.broadcasted_iota(jnp.int32, (sub, 2 * sub), 0)
    jcol = lax.broadcasted_iota(jnp.int32, (sub, 2 * sub), 1)
    later = ((jcol & (sub - 1)) > srow).astype(BF16)

    carry_scr[...] = jnp.zeros_like(carry_scr)
    acc_scr[...] = jnp.zeros_like(acc_scr)

    def block(jb, masked):
        start = pl.multiple_of(jb * tq, tq)
        k = k_ref[pl.ds(start, tq), :]
        v = v_ref[pl.ds(start, tq), :]
        z2 = _dot_nt(k, q)
        neg_abs = pltpu.bitcast(pltpu.bitcast(z2, jnp.uint32) | sign_bit, F32)
        sp2 = jnp.maximum(z2, 0.0) + jnp.log(1.0 + jnp.exp2(neg_abs)) * LOG2E
        if masked:
            key = lax.broadcasted_iota(jnp.int32, z2.shape, 0)
            qry = lax.broadcasted_iota(jnp.int32, z2.shape, 1)
            mask = key < qry
            sp2 = jnp.where(mask, sp2, 0.0)
        hi = sp2.astype(BF16)
        lo = (sp2 - hi.astype(F32)).astype(BF16)
        carry = carry_scr[...]
        parts = [None] * n_sub
        for c in reversed(range(n_sub)):
            rows = slice(c * sub, (c + 1) * sub)
            suffix = _dot(later, jnp.concatenate([hi[rows], lo[rows]], axis=0)) + carry
            a = jnp.exp2(z2[rows] - sp2[rows] - suffix)
            if masked:
                a = jnp.where(mask[rows], a, 0.0)
            parts[c] = a.astype(BF16)
            carry = carry + jnp.sum(sp2[rows], axis=0, keepdims=True)
        acc_scr[...] += _dot_tn(v, jnp.concatenate(parts, axis=0))
        carry_scr[...] = carry
        return jnp.min(carry)

    def keep_going(state):
        jb, min_carry = state
        return jnp.logical_and(jb >= 0, min_carry <= STICK_DEAD_LOG2)

    def body(state):
        jb, _ = state
        return jb - 1, block(jb, False)

    lax.while_loop(keep_going, body, (i - 1, block(i, True)))
    out_ref[...] = acc_scr[...].T.astype(out_ref.dtype)


def _stick_breaking(zmain, col_q, col_k, col_v, batch, seq, width):
    t = zmain.shape[0]
    dh = width // SB_HEADS
    tq = min(512, seq)
    nq = seq // tq
    per = width // dh
    return pl.pallas_call(
        functools.partial(_stick_kernel, sub=LANE),
        grid=(batch, SB_HEADS, nq),
        in_specs=[
            pl.BlockSpec((tq, dh), lambda b, h, i: (b * nq + i, col_q * per + h)),
            pl.BlockSpec((seq, dh), lambda b, h, i: (b, col_k * per + h)),
            pl.BlockSpec((seq, dh), lambda b, h, i: (b, col_v * per + h)),
        ],
        out_specs=pl.BlockSpec((tq, dh), lambda b, h, i: (b * nq + i, h)),
        out_shape=jax.ShapeDtypeStruct((t, width), BF16),
        scratch_shapes=[
            pltpu.VMEM((1, tq), F32),
            pltpu.VMEM((dh, tq), F32),
        ],
        compiler_params=_params("parallel", "parallel", "arbitrary"),
        name="stick_breaking",
    )(zmain, zmain, zmain)


def _merge_kernel(om_ref, od_ref, os_ref, wm_ref, wd_ref, ws_ref, gm_ref, gd_ref, gs_ref, o_ref, w_scr):
    @pl.when(pl.program_id(1) == 0)
    def _():
        for br, w_ref in enumerate((wm_ref, wd_ref, ws_ref)):
            w_scr[br] = w_ref[...].astype(BF16)

    acc = jax.nn.sigmoid(gm_ref[...].astype(F32)) * _dot(om_ref[...], w_scr[0])
    acc = acc + jax.nn.sigmoid(gd_ref[...].astype(F32)) * _dot(od_ref[...], w_scr[1])
    acc = acc + jax.nn.sigmoid(gs_ref[...].astype(F32)) * _dot(os_ref[...], w_scr[2])
    o_ref[...] = acc.astype(o_ref.dtype)


def _merge(out_m, out_d, out_s, w_branch, layer, zmain, gate_col0, d_model):
    t, width = out_m.shape
    tm = min(1024, t)
    tn = 512
    nj = d_model // tn
    act_spec = pl.BlockSpec((tm, width), lambda j, i: (i, 0))
    w_spec = lambda br: pl.BlockSpec((None, None, width, tn), lambda j, i: (layer, br, 0, j))
    gate_blk0 = gate_col0 // tn
    gate_spec = lambda br: pl.BlockSpec((tm, tn), lambda j, i: (i, gate_blk0 + br * nj + j))
    return pl.pallas_call(
        _merge_kernel,
        grid=(nj, t // tm),
        in_specs=[act_spec, act_spec, act_spec, w_spec(0), w_spec(1), w_spec(2),
                  gate_spec(0), gate_spec(1), gate_spec(2)],
        out_specs=pl.BlockSpec((tm, tn), lambda j, i: (i, j)),
        out_shape=jax.ShapeDtypeStruct((t, d_model), BF16),
        scratch_shapes=[pltpu.VMEM((3, width, tn), BF16)],
        compiler_params=_params("parallel", "arbitrary"),
        name="branch_merge",
    )(out_m, out_d, out_s, w_branch, w_branch, w_branch, zmain, zmain, zmain)


def _proj_residual_kernel(a_ref, w_ref, x_ref, gt_ref, o_ref, w_scr):
    @pl.when(pl.program_id(1) == 0)
    def _():
        w_scr[...] = w_ref[...].astype(BF16)

    o_ref[...] = x_ref[...] + gt_ref[...] * _dot(a_ref[...], w_scr[...])


def _proj_residual(a, w, layer, x2, mod, j_gate, seq, tm, tn, name):
    t, k = a.shape
    n = w.shape[2]
    tm = min(tm, seq)
    per_b = seq // tm
    return pl.pallas_call(
        _proj_residual_kernel,
        grid=(n // tn, t // tm),
        in_specs=[
            pl.BlockSpec((tm, k), lambda j, i: (i, 0)),
            pl.BlockSpec((None, k, tn), lambda j, i: (layer, 0, j)),
            pl.BlockSpec((tm, tn), lambda j, i: (i, j)),
            pl.BlockSpec((None, None, 1, tn), lambda j, i: (i // per_b, j_gate, 0, j)),
        ],
        out_specs=pl.BlockSpec((tm, tn), lambda j, i: (i, j)),
        out_shape=jax.ShapeDtypeStruct((t, n), F32),
        scratch_shapes=[pltpu.VMEM((k, tn), BF16)],
        compiler_params=_params("parallel", "arbitrary"),
        name=name,
    )(a, w, x2, mod)


def _ffn_up_kernel(h_ref, wg_ref, wu_ref, wc_ref, o_ref, tail_scr, wg_scr, wu_scr, *, tiles_per_seq):
    tm = h_ref.shape[0]
    i = pl.program_id(1)

    @pl.when(i == 0)
    def _():
        wg_scr[...] = wg_ref[...].astype(BF16)
        wu_scr[...] = wu_ref[...].astype(BF16)

    @pl.when(i % tiles_per_seq == 0)
    def _():
        tail_scr[...] = jnp.zeros_like(tail_scr)

    h = h_ref[...]
    gate = _dot(h, wg_scr[...])
    up = _dot(h, wu_scr[...])
    wc = wc_ref[...]
    xp = jnp.concatenate([tail_scr[...], gate], axis=0)
    conv = gate * wc[FF_CONV - 1:FF_CONV]
    for back in range(1, FF_CONV):
        conv = conv + pltpu.roll(xp, back, 0)[CONV_HALO:] * wc[FF_CONV - 1 - back:FF_CONV - back]
    tail_scr[...] = gate[tm - CONV_HALO:]
    o_ref[...] = (_silu(conv) * up).astype(o_ref.dtype)


def _ffn_up(h2, w_up, w_ffconv, layer, seq):
    t, k = h2.shape
    d_ff = w_ffconv.shape[2]
    tm = min(1024, seq)
    tn = 512
    nj = d_ff // tn
    return pl.pallas_call(
        functools.partial(_ffn_up_kernel, tiles_per_seq=seq // tm),
        grid=(nj, t // tm),
        in_specs=[
            pl.BlockSpec((tm, k), lambda j, i: (i, 0)),
            pl.BlockSpec((None, k, tn), lambda j, i: (layer, 0, j)),
            pl.BlockSpec((None, k, tn), lambda j, i: (layer, 0, nj + j)),
            pl.BlockSpec((None, FF_CONV, tn), lambda j, i: (layer, 0, j)),
        ],
        out_specs=pl.BlockSpec((tm, tn), lambda j, i: (i, j)),
        out_shape=jax.ShapeDtypeStruct((t, d_ff), BF16),
        scratch_shapes=[pltpu.VMEM((CONV_HALO, tn), F32), pltpu.VMEM((k, tn), BF16),
                        pltpu.VMEM((k, tn), BF16)],
        compiler_params=_params("parallel", "arbitrary"),
        name="ffn_up",
    )(h2, w_up, w_up, w_ffconv)


def kernel(x, c, w_ada, b_ada, g_mix, g_ffn, w_in, b_gate_if, w_mconv, g_mout, g_dq, g_dk,
           lam_q1, lam_k1, lam_q2, lam_k2, g_dsub, w_branch, w_out, w_up, w_ffconv, w_down):
    batch, seq, d = x.shape
    depth = w_ada.shape[0]
    width = d // 2
    t = batch * seq
    n_gate = 2 * M_HEADS

    mod_all = _ada_mod(c, w_ada, b_ada)
    x2 = x.reshape(t, d)

    for l in range(depth):
        mod = mod_all[l]
        gate0 = 4 * width
        sb_scale = (width // SB_HEADS) ** -0.5 * LOG2E
        w_main = _regroup_in_weights(w_in, l, width, n_gate, 7, sb_scale)
        b_if = jnp.zeros((1, LANE), F32).at[0, :n_gate].set(b_gate_if[l])

        h = _norm_mod(x2, g_mix[l], mod, 1, 0, seq)
        zmain = _matmul(h, w_main, BF16, 1024, 1024, "in_proj")
        zif = _gate_proj(h, w_in, l, gate0, b_if, n_gate)

        out_m = _mlstm(zmain, zif, w_mconv[l], g_mout[l], batch, seq, width)
        qn, kn = _qk_norm(zmain, g_dq[l], g_dk[l], width, 4, 5, seq)
        lam_init = 0.8 - 0.6 * math.exp(-0.3 * l)
        out_d = _diff_attn(qn, kn, zmain, 6, (lam_q1[l], lam_k1[l], lam_q2[l], lam_k2[l]),
                           g_dsub[l], lam_init, batch, seq, width)
        out_s = _stick_breaking(zmain, 7, 8, 9, batch, seq, width)

        merged = _merge(out_m, out_d, out_s, w_branch, l, zmain, 10 * width, d)
        x2 = _proj_residual(merged, w_out, l, x2, mod, 2, seq, 1024, 1024, "out_proj")

        h2 = _norm_mod(x2, g_ffn[l], mod, 4, 3, seq)
        act = _ffn_up(h2, w_up, w_ffconv, l, seq)
        x2 = _proj_residual(act, w_down, l, x2, mod, 5, seq, 512, 512, "ffn_down")

    return x2.reshape(batch, seq, d)
```

```python
import functools
import math

import jax
import jax.numpy as jnp
from jax import lax
from jax.experimental import pallas as pl
from jax.experimental.pallas import tpu as pltpu

EPS = 1e-6
M_HEADS = 4
M_CONV = 4
M_CHUNK = 128
DA_HEADS = 8
SB_HEADS = 8
N_MOD = 6
FF_CONV = 3
LANE = 128
CONV_HALO = 8
VMEM_LIMIT_BYTES = 56 * 1024 * 1024
HIGHEST = lax.Precision.HIGHEST
NEG_INF = float("-inf")
LOG2E = math.log2(math.e)
STICK_DEAD_LOG2 = 200.0

BF16 = jnp.bfloat16
F32 = jnp.float32


def _params(*sem):
    return pltpu.CompilerParams(dimension_semantics=sem, vmem_limit_bytes=VMEM_LIMIT_BYTES)


def _dot(a, b):
    return jnp.dot(a, b, preferred_element_type=F32)


def _dot_nt(a, b):
    return lax.dot_general(a, b, (((1,), (1,)), ((), ())), preferred_element_type=F32)


def _dot_tn(a, b):
    return lax.dot_general(a, b, (((0,), (0,)), ((), ())), preferred_element_type=F32)


def _silu(x):
    return x * jax.nn.sigmoid(x)


def _softplus(x):
    return jnp.maximum(x, 0.0) + jnp.log1p(jnp.exp(-jnp.abs(x)))


def _log_sigmoid(x):
    return -_softplus(-x)


def _ada_kernel(c_ref, w_ref, b_ref, o_ref):
    c = c_ref[...]
    o_ref[...] = jnp.dot(_silu(c), w_ref[...], preferred_element_type=F32,
                         precision=HIGHEST) + b_ref[...]


def _ada_mod(c, w_ada, b_ada):
    n_layers, d, n = w_ada.shape
    b = c.shape[0]
    rows = 8
    c_pad = jnp.zeros((rows, d), F32).at[:b].set(c)
    tn = 1024
    out = pl.pallas_call(
        _ada_kernel,
        grid=(n_layers, n // tn),
        in_specs=[
            pl.BlockSpec((rows, d), lambda l, j: (0, 0)),
            pl.BlockSpec((None, d, tn), lambda l, j: (l, 0, j)),
            pl.BlockSpec((None, 1, tn), lambda l, j: (l, 0, j)),
        ],
        out_specs=pl.BlockSpec((None, rows, tn), lambda l, j: (l, 0, j)),
        out_shape=jax.ShapeDtypeStruct((n_layers, rows, n), F32),
        compiler_params=_params("parallel", "parallel"),
        name="ada_mod",
    )(c_pad, w_ada, b_ada.reshape(n_layers, 1, n))
    return out[:, :b].reshape(n_layers, b, N_MOD, 1, d)


def _norm_mod_kernel(x_ref, g_ref, sc_ref, sh_ref, o_ref):
    x = x_ref[...]
    y = x * lax.rsqrt(jnp.mean(x * x, axis=-1, keepdims=True) + EPS) * g_ref[...]
    o_ref[...] = (y * (1.0 + sc_ref[...]) + sh_ref[...]).astype(o_ref.dtype)


def _norm_mod(x2, g, mod, j_scale, j_shift, seq):
    t, d = x2.shape
    ts = min(512, seq)
    per_b = seq // ts
    return pl.pallas_call(
        _norm_mod_kernel,
        grid=(t // ts,),
        in_specs=[
            pl.BlockSpec((ts, d), lambda i: (i, 0)),
            pl.BlockSpec((1, d), lambda i: (0, 0)),
            pl.BlockSpec((None, None, 1, d), lambda i: (i // per_b, j_scale, 0, 0)),
            pl.BlockSpec((None, None, 1, d), lambda i: (i // per_b, j_shift, 0, 0)),
        ],
        out_specs=pl.BlockSpec((ts, d), lambda i: (i, 0)),
        out_shape=jax.ShapeDtypeStruct((t, d), BF16),
        compiler_params=_params("parallel"),
        name="norm_mod",
    )(x2, g.reshape(1, d), mod, mod)


def _in_proj_kernel(h_ref, wt_ref, o_ref, w_scr, *, scaled_block, scale):
    m = pl.program_id(0)

    @pl.when(pl.program_id(1) == 0)
    def _():
        w = wt_ref[0] * jnp.where(m == scaled_block, scale, 1.0)
        w_scr[...] = w.T.astype(BF16)

    o_ref[...] = _dot(h_ref[...], w_scr[...]).astype(o_ref.dtype)


def _in_proj(h, w_in_t, layer, width, n_gate, scaled_block, scale):
    t, k = h.shape
    n_out = w_in_t.shape[1] - n_gate
    tm = min(1024, t)
    tn = width
    aligned_blocks = 4
    row0 = lambda m: pl.multiple_of(m * tn + jnp.where(m >= aligned_blocks, n_gate, 0), n_gate)
    return pl.pallas_call(
        functools.partial(_in_proj_kernel, scaled_block=scaled_block, scale=scale),
        grid=(n_out // tn, t // tm),
        in_specs=[
            pl.BlockSpec((tm, k), lambda m, i: (i, 0)),
            pl.BlockSpec((pl.Element(1), pl.Element(tn), pl.Element(k)),
                         lambda m, i: (layer, row0(m), 0)),
        ],
        out_specs=pl.BlockSpec((tm, tn), lambda m, i: (i, m)),
        out_shape=jax.ShapeDtypeStruct((t, n_out), BF16),
        scratch_shapes=[pltpu.VMEM((k, tn), BF16)],
        compiler_params=_params("parallel", "arbitrary"),
        name="in_proj",
    )(h, w_in_t)


def _gate_proj_kernel(a_ref, wt_ref, b_ref, o_ref):
    wt = wt_ref[...]
    pad = jnp.zeros((o_ref.shape[1] - wt.shape[0], wt.shape[1]), wt.dtype)
    w = jnp.concatenate([wt, pad], axis=0).astype(BF16)
    o_ref[...] = _dot_nt(a_ref[...], w) + b_ref[...]


def _gate_proj(a, w_in_t, layer, gate0, bias, n_gate):
    t, k = a.shape
    tm = min(1024, t)
    return pl.pallas_call(
        _gate_proj_kernel,
        grid=(t // tm,),
        in_specs=[
            pl.BlockSpec((tm, k), lambda i: (i, 0)),
            pl.BlockSpec((None, n_gate, k), lambda i: (layer, gate0 // n_gate, 0)),
            pl.BlockSpec((1, LANE), lambda i: (0, 0)),
        ],
        out_specs=pl.BlockSpec((tm, LANE), lambda i: (i, 0)),
        out_shape=jax.ShapeDtypeStruct((t, LANE), F32),
        compiler_params=_params("parallel"),
        name="gate_proj",
    )(a, w_in_t, bias)


def _mlstm_kernel(q_ref, k_ref, v_ref, o_ref, zif_ref, wconv_ref, g_ref, out_ref,
                  c_scr, n_scr, m_scr, qtail_scr, ktail_scr, *, dh):
    chunk = q_ref.shape[0]
    width = q_ref.shape[1]
    heads = width // dh

    @pl.when(pl.program_id(1) == 0)
    def _():
        c_scr[...] = jnp.zeros_like(c_scr)
        n_scr[...] = jnp.zeros_like(n_scr)
        m_scr[...] = jnp.zeros_like(m_scr)
        qtail_scr[...] = jnp.zeros_like(qtail_scr)
        ktail_scr[...] = jnp.zeros_like(ktail_scr)

    wconv = wconv_ref[...]

    def conv_silu(raw, tail_scr, w):
        xp = jnp.concatenate([tail_scr[...], raw], axis=0)
        acc = raw * w[M_CONV - 1:M_CONV]
        for back in range(1, M_CONV):
            shifted = pltpu.roll(xp, back, 0)[CONV_HALO:]
            acc = acc + shifted * w[M_CONV - 1 - back:M_CONV - back]
        tail_scr[...] = raw[chunk - CONV_HALO:]
        return _silu(acc)

    q_all = conv_silu(q_ref[...].astype(F32), qtail_scr, wconv[:, :width]) * (dh ** -0.5)
    k_all = conv_silu(k_ref[...].astype(F32), ktail_scr, wconv[:, width:])

    zif = zif_ref[...]
    lsf = _log_sigmoid(zif)
    zif_t = zif.T
    lsf_t = lsf.T

    row = lax.broadcasted_iota(jnp.int32, (chunk, chunk), 0)
    col = lax.broadcasted_iota(jnp.int32, (chunk, chunk), 1)
    causal = col <= row
    tri = causal.astype(F32)
    tri_t = (row <= col).astype(F32)

    for h in range(heads):
        sl = slice(h * dh, (h + 1) * dh)
        qh = q_all[:, sl]
        kh = k_all[:, sl]
        qb = qh.astype(BF16)
        kb = kh.astype(BF16)
        vb = v_ref[:, sl]

        i_col = zif[:, h:h + 1]
        i_row = zif_t[h:h + 1, :]
        lf_col = lsf[:, heads + h:heads + h + 1]
        lf_row = lsf_t[heads + h:heads + h + 1, :]

        b_colb = jnp.dot(tri, jnp.broadcast_to(lf_col, (chunk, chunk)),
                         preferred_element_type=F32, precision=HIGHEST)
        b_rowb = jnp.dot(jnp.broadcast_to(lf_row, (chunk, chunk)), tri_t,
                         preferred_element_type=F32, precision=HIGHEST)
        b_col = b_colb[:, :1]
        b_row = b_rowb[:1, :]
        b_last = b_row[:, chunk - 1:chunk]

        m_prev = m_scr[h][:, :1]
        dmat = jnp.where(causal, b_colb - b_rowb + i_row, NEG_INF)
        m_inter = b_col + m_prev
        m_t = jnp.maximum(jnp.max(dmat, axis=-1, keepdims=True), m_inter)
        w = _dot_nt(qb, kb) * jnp.exp(dmat - m_t)
        decay = jnp.exp(m_inter - m_t)
        c_prev = c_scr[h]
        n_prev = n_scr[h]
        num = _dot(w.astype(BF16), vb) + decay * _dot(qb, c_prev.astype(BF16))
        den = jnp.sum(w, axis=-1, keepdims=True) + decay * jnp.sum(qh * n_prev, axis=-1, keepdims=True)
        hh = num / jnp.maximum(jnp.abs(den), jnp.exp(-m_t))

        hn = hh * lax.rsqrt(jnp.mean(hh * hh, axis=-1, keepdims=True) + EPS) * g_ref[:, sl]
        out_ref[:, sl] = (hn * jax.nn.sigmoid(o_ref[:, sl].astype(F32))).astype(out_ref.dtype)

        g_col = b_last - b_col + i_col
        m_new = jnp.maximum(b_last + m_prev, jnp.max(g_col, axis=0, keepdims=True))
        wk = jnp.exp(g_col - m_new)
        carry_decay = jnp.exp(b_last + m_prev - m_new)
        kw = kh * wk
        c_scr[h] = carry_decay * c_prev + _dot_tn(kw.astype(BF16), vb)
        n_scr[h] = carry_decay * n_prev + jnp.sum(kw, axis=0, keepdims=True)
        m_scr[h] = jnp.broadcast_to(m_new, m_scr.shape[1:])


def _mlstm(zmain, zif, w_mconv, g_mout, batch, seq, width):
    t = zmain.shape[0]
    dh = width // M_HEADS
    nc = seq // M_CHUNK
    row_map = lambda cb: (lambda b, c: (b * nc + c, cb))
    return pl.pallas_call(
        functools.partial(_mlstm_kernel, dh=dh),
        grid=(batch, nc),
        in_specs=[
            pl.BlockSpec((M_CHUNK, width), row_map(0)),
            pl.BlockSpec((M_CHUNK, width), row_map(1)),
            pl.BlockSpec((M_CHUNK, width), row_map(2)),
            pl.BlockSpec((M_CHUNK, width), row_map(3)),
            pl.BlockSpec((M_CHUNK, LANE), row_map(0)),
            pl.BlockSpec((M_CONV, 2 * width), lambda b, c: (0, 0)),
            pl.BlockSpec((1, width), lambda b, c: (0, 0)),
        ],
        out_specs=pl.BlockSpec((M_CHUNK, width), row_map(0)),
        out_shape=jax.ShapeDtypeStruct((t, width), BF16),
        scratch_shapes=[
            pltpu.VMEM((M_HEADS, dh, dh), F32),
            pltpu.VMEM((M_HEADS, 1, dh), F32),
            pltpu.VMEM((M_HEADS, 1, LANE), F32),
            pltpu.VMEM((CONV_HALO, width), F32),
            pltpu.VMEM((CONV_HALO, width), F32),
        ],
        compiler_params=_params("parallel", "arbitrary"),
        name="mlstm",
    )(zmain, zmain, zmain, zmain, zif, w_mconv, g_mout.reshape(1, width))


def _qk_norm_kernel(q_ref, k_ref, gq_ref, gk_ref, qo_ref, ko_ref, *, dh):
    width = q_ref.shape[1]
    row = lax.broadcasted_iota(jnp.int32, (LANE, LANE), 0)
    col = lax.broadcasted_iota(jnp.int32, (LANE, LANE), 1)
    same_group = ((row // dh) == (col // dh)).astype(BF16)

    def group_rms(x_ref, g_ref, o_ref, scale):
        for tile in range(width // LANE):
            sl = slice(tile * LANE, (tile + 1) * LANE)
            x = x_ref[:, sl].astype(F32)
            sq = x * x
            hi = sq.astype(BF16)
            lo = (sq - hi.astype(F32)).astype(BF16)
            ssum = _dot(hi, same_group) + _dot(lo, same_group)
            y = x * lax.rsqrt(ssum * (1.0 / dh) + EPS) * g_ref[:, sl]
            o_ref[:, sl] = (y * scale).astype(o_ref.dtype)

    group_rms(q_ref, gq_ref, qo_ref, dh ** -0.5 * LOG2E)
    group_rms(k_ref, gk_ref, ko_ref, 1.0)


def _qk_norm(zmain, g_dq, g_dk, width, col_q, col_k, seq):
    t = zmain.shape[0]
    dh = g_dq.shape[0]
    ts = min(512, seq)
    reps = width // dh
    return pl.pallas_call(
        functools.partial(_qk_norm_kernel, dh=dh),
        grid=(t // ts,),
        in_specs=[
            pl.BlockSpec((ts, width), lambda i: (i, col_q)),
            pl.BlockSpec((ts, width), lambda i: (i, col_k)),
            pl.BlockSpec((1, width), lambda i: (0, 0)),
            pl.BlockSpec((1, width), lambda i: (0, 0)),
        ],
        out_specs=[pl.BlockSpec((ts, width), lambda i: (i, 0)),
                   pl.BlockSpec((ts, width), lambda i: (i, 0))],
        out_shape=[jax.ShapeDtypeStruct((t, width), BF16), jax.ShapeDtypeStruct((t, width), BF16)],
        compiler_params=_params("parallel"),
        name="qk_norm",
    )(zmain, zmain, jnp.tile(g_dq, reps).reshape(1, width), jnp.tile(g_dk, reps).reshape(1, width))


def _diff_attn_kernel(q_ref, k_ref, v_ref, lq1_ref, lk1_ref, lq2_ref, lk2_ref, g_ref, out_ref,
                      m_scr, l_scr, acc_scr, s_scr, *, lam_init, dh):
    tq = q_ref.shape[0]
    i = pl.program_id(2)
    q = q_ref[...]
    lane = lax.broadcasted_iota(jnp.int32, q.shape, 1)
    zero = jnp.zeros_like(q)
    q_halves = (jnp.where(lane < dh, q, zero), jnp.where(lane >= dh, q, zero))

    m_scr[...] = jnp.full_like(m_scr, NEG_INF)
    l_scr[...] = jnp.zeros_like(l_scr)
    acc_scr[...] = jnp.zeros_like(acc_scr)

    def key_block(ref, j):
        return ref[pl.ds(pl.multiple_of(j * tq, tq), tq), :]

    def put_scores(j, slot):
        k = key_block(k_ref, j)
        for c in range(2):
            s_scr[slot, c] = _dot_nt(k, q_halves[c])

    def consume(j, slot, masked):
        v = key_block(v_ref, j)
        probs, alphas = [], []
        for c in range(2):
            s = s_scr[slot, c]
            if masked:
                key = lax.broadcasted_iota(jnp.int32, s.shape, 0)
                qry = lax.broadcasted_iota(jnp.int32, s.shape, 1)
                s = jnp.where(key <= qry, s, NEG_INF)
            m_prev = m_scr[c]
            m_new = jnp.maximum(m_prev, jnp.max(s, axis=0, keepdims=True))
            alpha = jnp.exp2(m_prev - m_new)
            p = jnp.exp2(s - m_new)
            l_scr[c] = alpha * l_scr[c] + jnp.sum(p, axis=0, keepdims=True)
            m_scr[c] = m_new
            probs.append(p.astype(BF16))
            alphas.append(alpha)
        for c in range(2):
            acc_scr[c] = alphas[c] * acc_scr[c] + _dot_tn(v, probs[c])

    put_scores(0, 0)

    def pair(p, carry):
        j = 2 * p
        put_scores(j + 1, 1)
        consume(j, 0, False)
        put_scores(j + 2, 0)
        consume(j + 1, 1, False)
        return carry

    lax.fori_loop(0, i // 2, pair, 0)

    @pl.when(i % 2 == 0)
    def _():
        consume(i, 0, True)

    @pl.when(i % 2 == 1)
    def _():
        put_scores(i, 1)
        consume(i - 1, 0, False)
        consume(i, 1, True)

    lam = (jnp.exp(jnp.sum(lq1_ref[...] * lk1_ref[...], axis=-1, keepdims=True))
           - jnp.exp(jnp.sum(lq2_ref[...] * lk2_ref[...], axis=-1, keepdims=True)) + lam_init)
    o = (acc_scr[0] / l_scr[0] - lam * (acc_scr[1] / l_scr[1])).T
    on = o * lax.rsqrt(jnp.mean(o * o, axis=-1, keepdims=True) + EPS) * g_ref[...]
    out_ref[...] = (on * (1.0 - lam_init)).astype(out_ref.dtype)


def _diff_attn(qn, kn, zmain, col_v, lam_params, g_dsub, lam_init, batch, seq, width):
    t = qn.shape[0]
    dh = lam_params[0].shape[0]
    hw = 2 * dh
    heads = width // hw
    tq = min(512, seq)
    nq = seq // tq
    v_col0 = col_v * (width // hw)
    lam_specs = [pl.BlockSpec((1, dh), lambda b, h, i: (0, 0)) for _ in range(4)]
    return pl.pallas_call(
        functools.partial(_diff_attn_kernel, lam_init=lam_init, dh=dh),
        grid=(batch, heads, nq),
        in_specs=[
            pl.BlockSpec((tq, hw), lambda b, h, i: (b * nq + i, h)),
            pl.BlockSpec((seq, hw), lambda b, h, i: (b, h)),
            pl.BlockSpec((seq, hw), lambda b, h, i: (b, v_col0 + h)),
            *lam_specs,
            pl.BlockSpec((1, hw), lambda b, h, i: (0, 0)),
        ],
        out_specs=pl.BlockSpec((tq, hw), lambda b, h, i: (b * nq + i, h)),
        out_shape=jax.ShapeDtypeStruct((t, width), BF16),
        scratch_shapes=[
            pltpu.VMEM((2, 1, tq), F32),
            pltpu.VMEM((2, 1, tq), F32),
            pltpu.VMEM((2, hw, tq), F32),
            pltpu.VMEM((2, 2, tq, tq), F32),
        ],
        compiler_params=_params("parallel", "parallel", "arbitrary"),
        name="diff_attn",
    )(qn, kn, zmain, *[p.reshape(1, dh) for p in lam_params], g_dsub.reshape(1, hw))


def _stick_kernel(q_ref, k_ref, v_ref, out_ref, carry_scr, acc_scr, *, sub):
    tq, dh = q_ref.shape
    i = pl.program_id(2)
    q = q_ref[...]
    n_sub = tq // sub
    sign_bit = jnp.uint32(1 << 31)

    srow = lax.broadcasted_iota(jnp.int32, (sub, 2 * sub), 0)
    jcol = lax.broadcasted_iota(jnp.int32, (sub, 2 * sub), 1)
    later = ((jcol & (sub - 1)) > srow).astype(BF16)

    carry_scr[...] = jnp.zeros_like(carry_scr)
    acc_scr[...] = jnp.zeros_like(acc_scr)

    def block(jb, masked):
        start = pl.multiple_of(jb * tq, tq)
        k = k_ref[pl.ds(start, tq), :]
        v = v_ref[pl.ds(start, tq), :]
        z2 = _dot_nt(k, q)
        neg_abs = pltpu.bitcast(pltpu.bitcast(z2, jnp.uint32) | sign_bit, F32)
        sp2 = jnp.maximum(z2, 0.0) + jnp.log(1.0 + jnp.exp2(neg_abs)) * LOG2E
        if masked:
            key = lax.broadcasted_iota(jnp.int32, z2.shape, 0)
            qry = lax.broadcasted_iota(jnp.int32, z2.shape, 1)
            mask = key < qry
            sp2 = jnp.where(mask, sp2, 0.0)
        hi = sp2.astype(BF16)
        lo = (sp2 - hi.astype(F32)).astype(BF16)
        carry = carry_scr[...]
        parts = [None] * n_sub
        for c in reversed(range(n_sub)):
            rows = slice(c * sub, (c + 1) * sub)
            suffix = _dot(later, jnp.concatenate([hi[rows], lo[rows]], axis=0)) + carry
            a = jnp.exp2(z2[rows] - sp2[rows] - suffix)
            if masked:
                a = jnp.where(mask[rows], a, 0.0)
            parts[c] = a.astype(BF16)
            carry = carry + jnp.sum(sp2[rows], axis=0, keepdims=True)
        acc_scr[...] += _dot_tn(v, jnp.concatenate(parts, axis=0))
        carry_scr[...] = carry
        return jnp.min(carry)

    def keep_going(state):
        jb, min_carry = state
        return jnp.logical_and(jb >= 0, min_carry <= STICK_DEAD_LOG2)

    def body(state):
        jb, _ = state
        return jb - 1, block(jb, False)

    lax.while_loop(keep_going, body, (i - 1, block(i, True)))
    out_ref[...] = acc_scr[...].T.astype(out_ref.dtype)


def _stick_breaking(zmain, col_q, col_k, col_v, batch, seq, width):
    t = zmain.shape[0]
    dh = width // SB_HEADS
    tq = min(512, seq)
    nq = seq // tq
    per = width // dh
    return pl.pallas_call(
        functools.partial(_stick_kernel, sub=LANE),
        grid=(batch, SB_HEADS, nq),
        in_specs=[
            pl.BlockSpec((tq, dh), lambda b, h, i: (b * nq + i, col_q * per + h)),
            pl.BlockSpec((seq, dh), lambda b, h, i: (b, col_k * per + h)),
            pl.BlockSpec((seq, dh), lambda b, h, i: (b, col_v * per + h)),
        ],
        out_specs=pl.BlockSpec((tq, dh), lambda b, h, i: (b * nq + i, h)),
        out_shape=jax.ShapeDtypeStruct((t, width), BF16),
        scratch_shapes=[
            pltpu.VMEM((1, tq), F32),
            pltpu.VMEM((dh, tq), F32),
        ],
        compiler_params=_params("parallel", "parallel", "arbitrary"),
        name="stick_breaking",
    )(zmain, zmain, zmain)


def _merge_kernel(om_ref, od_ref, os_ref, wm_ref, wd_ref, ws_ref, gm_ref, gd_ref, gs_ref, o_ref, w_scr):
    @pl.when(pl.program_id(1) == 0)
    def _():
        for br, w_ref in enumerate((wm_ref, wd_ref, ws_ref)):
            w_scr[br] = w_ref[...].astype(BF16)

    acc = jax.nn.sigmoid(gm_ref[...].astype(F32)) * _dot(om_ref[...], w_scr[0])
    acc = acc + jax.nn.sigmoid(gd_ref[...].astype(F32)) * _dot(od_ref[...], w_scr[1])
    acc = acc + jax.nn.sigmoid(gs_ref[...].astype(F32)) * _dot(os_ref[...], w_scr[2])
    o_ref[...] = acc.astype(o_ref.dtype)


def _merge(out_m, out_d, out_s, w_branch, layer, zmain, gate_col0, d_model):
    t, width = out_m.shape
    tm = min(1024, t)
    tn = 512
    nj = d_model // tn
    act_spec = pl.BlockSpec((tm, width), lambda j, i: (i, 0))
    w_spec = lambda br: pl.BlockSpec((None, None, width, tn), lambda j, i: (layer, br, 0, j))
    gate_blk0 = gate_col0 // tn
    gate_spec = lambda br: pl.BlockSpec((tm, tn), lambda j, i: (i, gate_blk0 + br * nj + j))
    return pl.pallas_call(
        _merge_kernel,
        grid=(nj, t // tm),
        in_specs=[act_spec, act_spec, act_spec, w_spec(0), w_spec(1), w_spec(2),
                  gate_spec(0), gate_spec(1), gate_spec(2)],
        out_specs=pl.BlockSpec((tm, tn), lambda j, i: (i, j)),
        out_shape=jax.ShapeDtypeStruct((t, d_model), BF16),
        scratch_shapes=[pltpu.VMEM((3, width, tn), BF16)],
        compiler_params=_params("parallel", "arbitrary"),
        name="branch_merge",
    )(out_m, out_d, out_s, w_branch, w_branch, w_branch, zmain, zmain, zmain)


def _proj_residual_kernel(a_ref, w_ref, x_ref, gt_ref, o_ref, w_scr):
    @pl.when(pl.program_id(1) == 0)
    def _():
        w_scr[...] = w_ref[...].astype(BF16)

    o_ref[...] = x_ref[...] + gt_ref[...] * _dot(a_ref[...], w_scr[...])


def _proj_residual(a, w, layer, x2, mod, j_gate, seq, tm, tn, name):
    t, k = a.shape
    n = w.shape[2]
    tm = min(tm, seq)
    per_b = seq // tm
    return pl.pallas_call(
        _proj_residual_kernel,
        grid=(n // tn, t // tm),
        in_specs=[
            pl.BlockSpec((tm, k), lambda j, i: (i, 0)),
            pl.BlockSpec((None, k, tn), lambda j, i: (layer, 0, j)),
            pl.BlockSpec((tm, tn), lambda j, i: (i, j)),
            pl.BlockSpec((None, None, 1, tn), lambda j, i: (i // per_b, j_gate, 0, j)),
        ],
        out_specs=pl.BlockSpec((tm, tn), lambda j, i: (i, j)),
        out_shape=jax.ShapeDtypeStruct((t, n), F32),
        scratch_shapes=[pltpu.VMEM((k, tn), BF16)],
        compiler_params=_params("parallel", "arbitrary"),
        name=name,
    )(a, w, x2, mod)


def _ffn_up_kernel(h_ref, wg_ref, wu_ref, wc_ref, o_ref, tail_scr, wg_scr, wu_scr, *, tiles_per_seq):
    tm = h_ref.shape[0]
    i = pl.program_id(1)

    @pl.when(i == 0)
    def _():
        wg_scr[...] = wg_ref[...].astype(BF16)
        wu_scr[...] = wu_ref[...].astype(BF16)

    @pl.when(i % tiles_per_seq == 0)
    def _():
        tail_scr[...] = jnp.zeros_like(tail_scr)

    h = h_ref[...]
    gate = _dot(h, wg_scr[...])
    up = _dot(h, wu_scr[...])
    wc = wc_ref[...]
    xp = jnp.concatenate([tail_scr[...], gate], axis=0)
    conv = gate * wc[FF_CONV - 1:FF_CONV]
    for back in range(1, FF_CONV):
        conv = conv + pltpu.roll(xp, back, 0)[CONV_HALO:] * wc[FF_CONV - 1 - back:FF_CONV - back]
    tail_scr[...] = gate[tm - CONV_HALO:]
    o_ref[...] = (_silu(conv) * up).astype(o_ref.dtype)


def _ffn_up(h2, w_up, w_ffconv, layer, seq):
    t, k = h2.shape
    d_ff = w_ffconv.shape[2]
    tm = min(1024, seq)
    tn = 512
    nj = d_ff // tn
    return pl.pallas_call(
        functools.partial(_ffn_up_kernel, tiles_per_seq=seq // tm),
        grid=(nj, t // tm),
        in_specs=[
            pl.BlockSpec((tm, k), lambda j, i: (i, 0)),
            pl.BlockSpec((None, k, tn), lambda j, i: (layer, 0, j)),
            pl.BlockSpec((None, k, tn), lambda j, i: (layer, 0, nj + j)),
            pl.BlockSpec((None, FF_CONV, tn), lambda j, i: (layer, 0, j)),
        ],
        out_specs=pl.BlockSpec((tm, tn), lambda j, i: (i, j)),
        out_shape=jax.ShapeDtypeStruct((t, d_ff), BF16),
        scratch_shapes=[pltpu.VMEM((CONV_HALO, tn), F32), pltpu.VMEM((k, tn), BF16),
                        pltpu.VMEM((k, tn), BF16)],
        compiler_params=_params("parallel", "arbitrary"),
        name="ffn_up",
    )(h2, w_up, w_up, w_ffconv)


def kernel(x, c, w_ada, b_ada, g_mix, g_ffn, w_in, b_gate_if, w_mconv, g_mout, g_dq, g_dk,
           lam_q1, lam_k1, lam_q2, lam_k2, g_dsub, w_branch, w_out, w_up, w_ffconv, w_down):
    batch, seq, d = x.shape
    depth = w_ada.shape[0]
    width = d // 2
    t = batch * seq
    n_gate = 2 * M_HEADS

    mod_all = _ada_mod(c, w_ada, b_ada)
    x2 = x.reshape(t, d)
    w_in_t = jnp.swapaxes(w_in, 1, 2)

    for l in range(depth):
        mod = mod_all[l]
        gate0 = 4 * width
        sb_scale = (width // SB_HEADS) ** -0.5 * LOG2E
        b_if = jnp.zeros((1, LANE), F32).at[0, :n_gate].set(b_gate_if[l])

        h = _norm_mod(x2, g_mix[l], mod, 1, 0, seq)
        zmain = _in_proj(h, w_in_t, l, width, n_gate, 7, sb_scale)
        zif = _gate_proj(h, w_in_t, l, gate0, b_if, n_gate)

        out_m = _mlstm(zmain, zif, w_mconv[l], g_mout[l], batch, seq, width)
        qn, kn = _qk_norm(zmain, g_dq[l], g_dk[l], width, 4, 5, seq)
        lam_init = 0.8 - 0.6 * math.exp(-0.3 * l)
        out_d = _diff_attn(qn, kn, zmain, 6, (lam_q1[l], lam_k1[l], lam_q2[l], lam_k2[l]),
                           g_dsub[l], lam_init, batch, seq, width)
        out_s = _stick_breaking(zmain, 7, 8, 9, batch, seq, width)

        merged = _merge(out_m, out_d, out_s, w_branch, l, zmain, 10 * width, d)
        x2 = _proj_residual(merged, w_out, l, x2, mod, 2, seq, 1024, 1024, "out_proj")

        h2 = _norm_mod(x2, g_ffn[l], mod, 4, 3, seq)
        act = _ffn_up(h2, w_up, w_ffconv, l, seq)
        x2 = _proj_residual(act, w_down, l, x2, mod, 5, seq, 512, 512, "ffn_down")

    return x2.reshape(batch, seq, d)
```

```python
import functools
import math

import jax
import jax.numpy as jnp
from jax import lax
from jax.experimental import pallas as pl
from jax.experimental.pallas import tpu as pltpu

EPS = 1e-6
M_HEADS = 4
M_CONV = 4
M_CHUNK = 128
DA_HEADS = 8
SB_HEADS = 8
N_MOD = 6
FF_CONV = 3
LANE = 128
CONV_HALO = 8
VMEM_LIMIT_BYTES = 56 * 1024 * 1024
HIGHEST = lax.Precision.HIGHEST
NEG_INF = float("-inf")
LOG2E = math.log2(math.e)
STICK_DEAD_LOG2 = 152.0

BF16 = jnp.bfloat16
F32 = jnp.float32


def _params(*sem):
    return pltpu.CompilerParams(dimension_semantics=sem, vmem_limit_bytes=VMEM_LIMIT_BYTES)


def _dot(a, b):
    return jnp.dot(a, b, preferred_element_type=F32)


def _dot_nt(a, b):
    return lax.dot_general(a, b, (((1,), (1,)), ((), ())), preferred_element_type=F32)


def _dot_tn(a, b):
    return lax.dot_general(a, b, (((0,), (0,)), ((), ())), preferred_element_type=F32)


def _silu(x):
    return x * jax.nn.sigmoid(x)


def _softplus(x):
    return jnp.maximum(x, 0.0) + jnp.log1p(jnp.exp(-jnp.abs(x)))


def _log_sigmoid(x):
    return -_softplus(-x)


def _ada_kernel(c_ref, w_ref, b_ref, o_ref):
    c = c_ref[...]
    o_ref[...] = jnp.dot(_silu(c), w_ref[...], preferred_element_type=F32,
                         precision=HIGHEST) + b_ref[...]


def _ada_mod(c, w_ada, b_ada):
    n_layers, d, n = w_ada.shape
    b = c.shape[0]
    rows = 8
    c_pad = jnp.zeros((rows, d), F32).at[:b].set(c)
    tn = 1024
    out = pl.pallas_call(
        _ada_kernel,
        grid=(n_layers, n // tn),
        in_specs=[
            pl.BlockSpec((rows, d), lambda l, j: (0, 0)),
            pl.BlockSpec((None, d, tn), lambda l, j: (l, 0, j)),
            pl.BlockSpec((None, 1, tn), lambda l, j: (l, 0, j)),
        ],
        out_specs=pl.BlockSpec((None, rows, tn), lambda l, j: (l, 0, j)),
        out_shape=jax.ShapeDtypeStruct((n_layers, rows, n), F32),
        compiler_params=_params("parallel", "parallel"),
        name="ada_mod",
    )(c_pad, w_ada, b_ada.reshape(n_layers, 1, n))
    return out[:, :b].reshape(n_layers, b, N_MOD, 1, d)


def _norm_mod_kernel(x_ref, g_ref, sc_ref, sh_ref, o_ref):
    x = x_ref[...]
    y = x * lax.rsqrt(jnp.mean(x * x, axis=-1, keepdims=True) + EPS) * g_ref[...]
    o_ref[...] = (y * (1.0 + sc_ref[...]) + sh_ref[...]).astype(o_ref.dtype)


def _norm_mod(x2, g, mod, j_scale, j_shift, seq):
    t, d = x2.shape
    ts = min(512, seq)
    per_b = seq // ts
    return pl.pallas_call(
        _norm_mod_kernel,
        grid=(t // ts,),
        in_specs=[
            pl.BlockSpec((ts, d), lambda i: (i, 0)),
            pl.BlockSpec((1, d), lambda i: (0, 0)),
            pl.BlockSpec((None, None, 1, d), lambda i: (i // per_b, j_scale, 0, 0)),
            pl.BlockSpec((None, None, 1, d), lambda i: (i // per_b, j_shift, 0, 0)),
        ],
        out_specs=pl.BlockSpec((ts, d), lambda i: (i, 0)),
        out_shape=jax.ShapeDtypeStruct((t, d), BF16),
        compiler_params=_params("parallel"),
        name="norm_mod",
    )(x2, g.reshape(1, d), mod, mod)


def _in_proj_kernel(h_ref, wt_ref, o_ref, w_scr, *, scaled_block, scale):
    m = pl.program_id(0)

    @pl.when(pl.program_id(1) == 0)
    def _():
        w = wt_ref[0] * jnp.where(m == scaled_block, scale, 1.0)
        w_scr[...] = w.T.astype(BF16)

    o_ref[...] = _dot(h_ref[...], w_scr[...]).astype(o_ref.dtype)


def _in_proj(h, w_in_t, layer, width, n_gate, scaled_block, scale):
    t, k = h.shape
    n_out = w_in_t.shape[1] - n_gate
    tm = min(1024, t)
    tn = width
    aligned_blocks = 4
    row0 = lambda m: pl.multiple_of(m * tn + jnp.where(m >= aligned_blocks, n_gate, 0), n_gate)
    return pl.pallas_call(
        functools.partial(_in_proj_kernel, scaled_block=scaled_block, scale=scale),
        grid=(n_out // tn, t // tm),
        in_specs=[
            pl.BlockSpec((tm, k), lambda m, i: (i, 0)),
            pl.BlockSpec((pl.Element(1), pl.Element(tn), pl.Element(k)),
                         lambda m, i: (layer, row0(m), 0)),
        ],
        out_specs=pl.BlockSpec((tm, tn), lambda m, i: (i, m)),
        out_shape=jax.ShapeDtypeStruct((t, n_out), BF16),
        scratch_shapes=[pltpu.VMEM((k, tn), BF16)],
        compiler_params=_params("parallel", "arbitrary"),
        name="in_proj",
    )(h, w_in_t)


def _gate_proj_kernel(a_ref, wt_ref, b_ref, o_ref):
    wt = wt_ref[...]
    pad = jnp.zeros((o_ref.shape[1] - wt.shape[0], wt.shape[1]), wt.dtype)
    w = jnp.concatenate([wt, pad], axis=0).astype(BF16)
    o_ref[...] = _dot_nt(a_ref[...], w) + b_ref[...]


def _gate_proj(a, w_in_t, layer, gate0, bias, n_gate):
    t, k = a.shape
    tm = min(1024, t)
    return pl.pallas_call(
        _gate_proj_kernel,
        grid=(t // tm,),
        in_specs=[
            pl.BlockSpec((tm, k), lambda i: (i, 0)),
            pl.BlockSpec((None, n_gate, k), lambda i: (layer, gate0 // n_gate, 0)),
            pl.BlockSpec((1, LANE), lambda i: (0, 0)),
        ],
        out_specs=pl.BlockSpec((tm, LANE), lambda i: (i, 0)),
        out_shape=jax.ShapeDtypeStruct((t, LANE), F32),
        compiler_params=_params("parallel"),
        name="gate_proj",
    )(a, w_in_t, bias)


def _mlstm_kernel(q_ref, k_ref, v_ref, o_ref, zif_ref, wconv_ref, g_ref, out_ref,
                  c_scr, n_scr, m_scr, qtail_scr, ktail_scr, *, dh):
    chunk = q_ref.shape[0]
    width = q_ref.shape[1]
    heads = width // dh

    @pl.when(pl.program_id(1) == 0)
    def _():
        c_scr[...] = jnp.zeros_like(c_scr)
        n_scr[...] = jnp.zeros_like(n_scr)
        m_scr[...] = jnp.zeros_like(m_scr)
        qtail_scr[...] = jnp.zeros_like(qtail_scr)
        ktail_scr[...] = jnp.zeros_like(ktail_scr)

    wconv = wconv_ref[...]

    def conv_silu(raw, tail_scr, w):
        xp = jnp.concatenate([tail_scr[...], raw], axis=0)
        acc = raw * w[M_CONV - 1:M_CONV]
        for back in range(1, M_CONV):
            shifted = pltpu.roll(xp, back, 0)[CONV_HALO:]
            acc = acc + shifted * w[M_CONV - 1 - back:M_CONV - back]
        tail_scr[...] = raw[chunk - CONV_HALO:]
        return _silu(acc)

    q_all = conv_silu(q_ref[...].astype(F32), qtail_scr, wconv[:, :width]) * (dh ** -0.5)
    k_all = conv_silu(k_ref[...].astype(F32), ktail_scr, wconv[:, width:])

    zif = zif_ref[...]
    lsf = _log_sigmoid(zif)
    zif_t = zif.T
    lsf_t = lsf.T

    row = lax.broadcasted_iota(jnp.int32, (chunk, chunk), 0)
    col = lax.broadcasted_iota(jnp.int32, (chunk, chunk), 1)
    causal = col <= row
    tri = causal.astype(F32)
    tri_t = (row <= col).astype(F32)

    for h in range(heads):
        sl = slice(h * dh, (h + 1) * dh)
        qh = q_all[:, sl]
        kh = k_all[:, sl]
        qb = qh.astype(BF16)
        kb = kh.astype(BF16)
        vb = v_ref[:, sl]

        i_col = zif[:, h:h + 1]
        i_row = zif_t[h:h + 1, :]
        lf_col = lsf[:, heads + h:heads + h + 1]
        lf_row = lsf_t[heads + h:heads + h + 1, :]

        b_colb = jnp.dot(tri, jnp.broadcast_to(lf_col, (chunk, chunk)),
                         preferred_element_type=F32, precision=HIGHEST)
        b_rowb = jnp.dot(jnp.broadcast_to(lf_row, (chunk, chunk)), tri_t,
                         preferred_element_type=F32, precision=HIGHEST)
        b_col = b_colb[:, :1]
        b_row = b_rowb[:1, :]
        b_last = b_row[:, chunk - 1:chunk]

        m_prev = m_scr[h][:, :1]
        dmat = jnp.where(causal, b_colb - b_rowb + i_row, NEG_INF)
        m_inter = b_col + m_prev
        m_t = jnp.maximum(jnp.max(dmat, axis=-1, keepdims=True), m_inter)
        w = _dot_nt(qb, kb) * jnp.exp(dmat - m_t)
        decay = jnp.exp(m_inter - m_t)
        c_prev = c_scr[h]
        n_prev = n_scr[h]
        num = _dot(w.astype(BF16), vb) + decay * _dot(qb, c_prev.astype(BF16))
        den = jnp.sum(w, axis=-1, keepdims=True) + decay * jnp.sum(qh * n_prev, axis=-1, keepdims=True)
        hh = num / jnp.maximum(jnp.abs(den), jnp.exp(-m_t))

        hn = hh * lax.rsqrt(jnp.mean(hh * hh, axis=-1, keepdims=True) + EPS) * g_ref[:, sl]
        out_ref[:, sl] = (hn * jax.nn.sigmoid(o_ref[:, sl].astype(F32))).astype(out_ref.dtype)

        g_col = b_last - b_col + i_col
        m_new = jnp.maximum(b_last + m_prev, jnp.max(g_col, axis=0, keepdims=True))
        wk = jnp.exp(g_col - m_new)
        carry_decay = jnp.exp(b_last + m_prev - m_new)
        kw = kh * wk
        c_scr[h] = carry_decay * c_prev + _dot_tn(kw.astype(BF16), vb)
        n_scr[h] = carry_decay * n_prev + jnp.sum(kw, axis=0, keepdims=True)
        m_scr[h] = jnp.broadcast_to(m_new, m_scr.shape[1:])


def _mlstm(zmain, zif, w_mconv, g_mout, batch, seq, width):
    t = zmain.shape[0]
    dh = width // M_HEADS
    nc = seq // M_CHUNK
    row_map = lambda cb: (lambda b, c: (b * nc + c, cb))
    return pl.pallas_call(
        functools.partial(_mlstm_kernel, dh=dh),
        grid=(batch, nc),
        in_specs=[
            pl.BlockSpec((M_CHUNK, width), row_map(0)),
            pl.BlockSpec((M_CHUNK, width), row_map(1)),
            pl.BlockSpec((M_CHUNK, width), row_map(2)),
            pl.BlockSpec((M_CHUNK, width), row_map(3)),
            pl.BlockSpec((M_CHUNK, LANE), row_map(0)),
            pl.BlockSpec((M_CONV, 2 * width), lambda b, c: (0, 0)),
            pl.BlockSpec((1, width), lambda b, c: (0, 0)),
        ],
        out_specs=pl.BlockSpec((M_CHUNK, width), row_map(0)),
        out_shape=jax.ShapeDtypeStruct((t, width), BF16),
        scratch_shapes=[
            pltpu.VMEM((M_HEADS, dh, dh), F32),
            pltpu.VMEM((M_HEADS, 1, dh), F32),
            pltpu.VMEM((M_HEADS, 1, LANE), F32),
            pltpu.VMEM((CONV_HALO, width), F32),
            pltpu.VMEM((CONV_HALO, width), F32),
        ],
        compiler_params=_params("parallel", "arbitrary"),
        name="mlstm",
    )(zmain, zmain, zmain, zmain, zif, w_mconv, g_mout.reshape(1, width))


def _qk_norm_kernel(q_ref, k_ref, gq_ref, gk_ref, qo_ref, ko_ref, *, dh):
    width = q_ref.shape[1]
    row = lax.broadcasted_iota(jnp.int32, (LANE, LANE), 0)
    col = lax.broadcasted_iota(jnp.int32, (LANE, LANE), 1)
    same_group = ((row // dh) == (col // dh)).astype(BF16)

    def group_rms(x_ref, g_ref, o_ref, scale):
        for tile in range(width // LANE):
            sl = slice(tile * LANE, (tile + 1) * LANE)
            x = x_ref[:, sl].astype(F32)
            sq = x * x
            hi = sq.astype(BF16)
            lo = (sq - hi.astype(F32)).astype(BF16)
            ssum = _dot(hi, same_group) + _dot(lo, same_group)
            y = x * lax.rsqrt(ssum * (1.0 / dh) + EPS) * g_ref[:, sl]
            o_ref[:, sl] = (y * scale).astype(o_ref.dtype)

    group_rms(q_ref, gq_ref, qo_ref, dh ** -0.5 * LOG2E)
    group_rms(k_ref, gk_ref, ko_ref, 1.0)


def _qk_norm(zmain, g_dq, g_dk, width, col_q, col_k, seq):
    t = zmain.shape[0]
    dh = g_dq.shape[0]
    ts = min(512, seq)
    reps = width // dh
    return pl.pallas_call(
        functools.partial(_qk_norm_kernel, dh=dh),
        grid=(t // ts,),
        in_specs=[
            pl.BlockSpec((ts, width), lambda i: (i, col_q)),
            pl.BlockSpec((ts, width), lambda i: (i, col_k)),
            pl.BlockSpec((1, width), lambda i: (0, 0)),
            pl.BlockSpec((1, width), lambda i: (0, 0)),
        ],
        out_specs=[pl.BlockSpec((ts, width), lambda i: (i, 0)),
                   pl.BlockSpec((ts, width), lambda i: (i, 0))],
        out_shape=[jax.ShapeDtypeStruct((t, width), BF16), jax.ShapeDtypeStruct((t, width), BF16)],
        compiler_params=_params("parallel"),
        name="qk_norm",
    )(zmain, zmain, jnp.tile(g_dq, reps).reshape(1, width), jnp.tile(g_dk, reps).reshape(1, width))


def _diff_attn_kernel(q_ref, k_ref, v_ref, lq1_ref, lk1_ref, lq2_ref, lk2_ref, g_ref, out_ref,
                      m_scr, l_scr, acc_scr, s_scr, *, lam_init, dh):
    tq = q_ref.shape[0]
    i = pl.program_id(2)
    q = q_ref[...]
    lane = lax.broadcasted_iota(jnp.int32, q.shape, 1)
    zero = jnp.zeros_like(q)
    q_halves = (jnp.where(lane < dh, q, zero), jnp.where(lane >= dh, q, zero))

    m_scr[...] = jnp.full_like(m_scr, NEG_INF)
    l_scr[...] = jnp.zeros_like(l_scr)
    acc_scr[...] = jnp.zeros_like(acc_scr)

    def key_block(ref, j):
        return ref[pl.ds(pl.multiple_of(j * tq, tq), tq), :]

    def put_scores(j, slot):
        k = key_block(k_ref, j)
        for c in range(2):
            s_scr[slot, c] = _dot_nt(k, q_halves[c])

    def consume(j, slot, masked):
        v = key_block(v_ref, j)
        probs, alphas = [], []
        for c in range(2):
            s = s_scr[slot, c]
            if masked:
                key = lax.broadcasted_iota(jnp.int32, s.shape, 0)
                qry = lax.broadcasted_iota(jnp.int32, s.shape, 1)
                s = jnp.where(key <= qry, s, NEG_INF)
            m_prev = m_scr[c]
            m_new = jnp.maximum(m_prev, jnp.max(s, axis=0, keepdims=True))
            alpha = jnp.exp2(m_prev - m_new)
            p = jnp.exp2(s - m_new)
            l_scr[c] = alpha * l_scr[c] + jnp.sum(p, axis=0, keepdims=True)
            m_scr[c] = m_new
            probs.append(p.astype(BF16))
            alphas.append(alpha)
        for c in range(2):
            acc_scr[c] = alphas[c] * acc_scr[c] + _dot_tn(v, probs[c])

    put_scores(0, 0)

    def pair(p, carry):
        j = 2 * p
        put_scores(j + 1, 1)
        consume(j, 0, False)
        put_scores(j + 2, 0)
        consume(j + 1, 1, False)
        return carry

    lax.fori_loop(0, i // 2, pair, 0)

    @pl.when(i % 2 == 0)
    def _():
        consume(i, 0, True)

    @pl.when(i % 2 == 1)
    def _():
        put_scores(i, 1)
        consume(i - 1, 0, False)
        consume(i, 1, True)

    lam = (jnp.exp(jnp.sum(lq1_ref[...] * lk1_ref[...], axis=-1, keepdims=True))
           - jnp.exp(jnp.sum(lq2_ref[...] * lk2_ref[...], axis=-1, keepdims=True)) + lam_init)
    o = (acc_scr[0] / l_scr[0] - lam * (acc_scr[1] / l_scr[1])).T
    on = o * lax.rsqrt(jnp.mean(o * o, axis=-1, keepdims=True) + EPS) * g_ref[...]
    out_ref[...] = (on * (1.0 - lam_init)).astype(out_ref.dtype)


def _diff_attn(qn, kn, zmain, col_v, lam_params, g_dsub, lam_init, batch, seq, width):
    t = qn.shape[0]
    dh = lam_params[0].shape[0]
    hw = 2 * dh
    heads = width // hw
    tq = min(512, seq)
    nq = seq // tq
    v_col0 = col_v * (width // hw)
    lam_specs = [pl.BlockSpec((1, dh), lambda b, h, i: (0, 0)) for _ in range(4)]
    return pl.pallas_call(
        functools.partial(_diff_attn_kernel, lam_init=lam_init, dh=dh),
        grid=(batch, heads, nq),
        in_specs=[
            pl.BlockSpec((tq, hw), lambda b, h, i: (b * nq + i, h)),
            pl.BlockSpec((seq, hw), lambda b, h, i: (b, h)),
            pl.BlockSpec((seq, hw), lambda b, h, i: (b, v_col0 + h)),
            *lam_specs,
            pl.BlockSpec((1, hw), lambda b, h, i: (0, 0)),
        ],
        out_specs=pl.BlockSpec((tq, hw), lambda b, h, i: (b * nq + i, h)),
        out_shape=jax.ShapeDtypeStruct((t, width), BF16),
        scratch_shapes=[
            pltpu.VMEM((2, 1, tq), F32),
            pltpu.VMEM((2, 1, tq), F32),
            pltpu.VMEM((2, hw, tq), F32),
            pltpu.VMEM((2, 2, tq, tq), F32),
        ],
        compiler_params=_params("parallel", "parallel", "arbitrary"),
        name="diff_attn",
    )(qn, kn, zmain, *[p.reshape(1, dh) for p in lam_params], g_dsub.reshape(1, hw))


def _stick_kernel(q_ref, k_ref, v_ref, out_ref, carry_scr, acc_scr, *, sub, far):
    tq, dh = q_ref.shape
    i = pl.program_id(2)
    sign_bit = jnp.uint32(1 << 31)

    srow = lax.broadcasted_iota(jnp.int32, (sub, 2 * sub), 0)
    jcol = lax.broadcasted_iota(jnp.int32, (sub, 2 * sub), 1)
    later = ((jcol & (sub - 1)) > srow).astype(BF16)

    q = q_ref[...]

    def softplus2(z2):
        neg_abs = pltpu.bitcast(pltpu.bitcast(z2, jnp.uint32) | sign_bit, F32)
        return jnp.maximum(z2, 0.0) + jnp.log(1.0 + jnp.exp2(neg_abs)) * LOG2E

    def suffix_in_block(sp2):
        hi = sp2.astype(BF16)
        lo = (sp2 - hi.astype(F32)).astype(BF16)
        return _dot(later, jnp.concatenate([hi, lo], axis=0))

    start = pl.multiple_of(i * tq, tq)
    z_tile = _dot_nt(k_ref[pl.ds(start, tq), :], q)
    carry = jnp.zeros((1, tq), F32)
    key_rows = []
    for c in reversed(range(tq // sub)):
        lane0 = c * sub
        z2 = z_tile[lane0:lane0 + sub, lane0:]
        before = (lax.broadcasted_iota(jnp.int32, z2.shape, 0)
                  < lax.broadcasted_iota(jnp.int32, z2.shape, 1))
        sp2 = jnp.where(before, softplus2(z2), 0.0)
        a = jnp.exp2(z2 - sp2 - (suffix_in_block(sp2) + carry[:, lane0:]))
        a = jnp.where(before, a, 0.0).astype(BF16)
        total = jnp.sum(sp2, axis=0, keepdims=True)
        if lane0:
            a = jnp.concatenate([jnp.zeros((sub, lane0), BF16), a], axis=1)
            total = jnp.concatenate([jnp.zeros((1, lane0), F32), total], axis=1)
        key_rows.insert(0, a)
        carry = carry + total
    acc = _dot_tn(v_ref[pl.ds(start, tq), :], jnp.concatenate(key_rows, axis=0))

    def far_keys(jb, carry, valid):
        start = pl.multiple_of(jb * far, far)
        z2 = _dot_nt(k_ref[pl.ds(start, far), :], q)
        sp2 = softplus2(z2)
        if valid is not None:
            sp2 = jnp.where(valid, sp2, 0.0)
        parts = []
        for c in reversed(range(far // sub)):
            rows = slice(c * sub, (c + 1) * sub)
            a = jnp.exp2(z2[rows] - sp2[rows] - (suffix_in_block(sp2[rows]) + carry))
            if valid is not None:
                a = jnp.where(valid, a, 0.0)
            parts.insert(0, a.astype(BF16))
            carry = carry + jnp.sum(sp2[rows], axis=0, keepdims=True)
        return _dot_tn(v_ref[pl.ds(start, far), :], jnp.concatenate(parts, axis=0)), carry

    jb = i * (tq // far) - 1
    near, carry = far_keys(jnp.maximum(jb, 0), carry, i > 0)
    acc_scr[...] = acc + near
    carry_scr[...] = carry

    def keep_going(state):
        jb, min_carry = state
        return jnp.logical_and(jb >= 0, min_carry <= STICK_DEAD_LOG2)

    def body(state):
        jb, _ = state
        more, carry = far_keys(jb, carry_scr[...], None)
        acc_scr[...] += more
        carry_scr[...] = carry
        return jb - 1, jnp.min(carry)

    lax.while_loop(keep_going, body, (jb - 1, jnp.min(carry)))
    out_ref[...] = acc_scr[...].T.astype(out_ref.dtype)


def _stick_breaking(zmain, col_q, col_k, col_v, batch, seq, width):
    t = zmain.shape[0]
    dh = width // SB_HEADS
    tq = min(512, seq)
    nq = seq // tq
    per = width // dh
    return pl.pallas_call(
        functools.partial(_stick_kernel, sub=LANE, far=min(256, seq)),
        grid=(batch, SB_HEADS, nq),
        in_specs=[
            pl.BlockSpec((tq, dh), lambda b, h, i: (b * nq + i, col_q * per + h)),
            pl.BlockSpec((seq, dh), lambda b, h, i: (b, col_k * per + h)),
            pl.BlockSpec((seq, dh), lambda b, h, i: (b, col_v * per + h)),
        ],
        out_specs=pl.BlockSpec((tq, dh), lambda b, h, i: (b * nq + i, h)),
        out_shape=jax.ShapeDtypeStruct((t, width), BF16),
        scratch_shapes=[
            pltpu.VMEM((1, tq), F32),
            pltpu.VMEM((dh, tq), F32),
        ],
        compiler_params=_params("parallel", "parallel", "arbitrary"),
        name="stick_breaking",
    )(zmain, zmain, zmain)


def _merge_kernel(om_ref, od_ref, os_ref, wm_ref, wd_ref, ws_ref, gm_ref, gd_ref, gs_ref, o_ref, w_scr):
    @pl.when(pl.program_id(1) == 0)
    def _():
        for br, w_ref in enumerate((wm_ref, wd_ref, ws_ref)):
            w_scr[br] = w_ref[...].astype(BF16)

    acc = jax.nn.sigmoid(gm_ref[...].astype(F32)) * _dot(om_ref[...], w_scr[0])
    acc = acc + jax.nn.sigmoid(gd_ref[...].astype(F32)) * _dot(od_ref[...], w_scr[1])
    acc = acc + jax.nn.sigmoid(gs_ref[...].astype(F32)) * _dot(os_ref[...], w_scr[2])
    o_ref[...] = acc.astype(o_ref.dtype)


def _merge(out_m, out_d, out_s, w_branch, layer, zmain, gate_col0, d_model):
    t, width = out_m.shape
    tm = min(1024, t)
    tn = 512
    nj = d_model // tn
    act_spec = pl.BlockSpec((tm, width), lambda j, i: (i, 0))
    w_spec = lambda br: pl.BlockSpec((None, None, width, tn), lambda j, i: (layer, br, 0, j))
    gate_blk0 = gate_col0 // tn
    gate_spec = lambda br: pl.BlockSpec((tm, tn), lambda j, i: (i, gate_blk0 + br * nj + j))
    return pl.pallas_call(
        _merge_kernel,
        grid=(nj, t // tm),
        in_specs=[act_spec, act_spec, act_spec, w_spec(0), w_spec(1), w_spec(2),
                  gate_spec(0), gate_spec(1), gate_spec(2)],
        out_specs=pl.BlockSpec((tm, tn), lambda j, i: (i, j)),
        out_shape=jax.ShapeDtypeStruct((t, d_model), BF16),
        scratch_shapes=[pltpu.VMEM((3, width, tn), BF16)],
        compiler_params=_params("parallel", "arbitrary"),
        name="branch_merge",
    )(out_m, out_d, out_s, w_branch, w_branch, w_branch, zmain, zmain, zmain)


def _proj_residual_kernel(a_ref, w_ref, x_ref, gt_ref, o_ref, w_scr):
    @pl.when(pl.program_id(1) == 0)
    def _():
        w_scr[...] = w_ref[...].astype(BF16)

    o_ref[...] = x_ref[...] + gt_ref[...] * _dot(a_ref[...], w_scr[...])


def _proj_residual(a, w, layer, x2, mod, j_gate, seq, tm, tn, name):
    t, k = a.shape
    n = w.shape[2]
    tm = min(tm, seq)
    per_b = seq // tm
    return pl.pallas_call(
        _proj_residual_kernel,
        grid=(n // tn, t // tm),
        in_specs=[
            pl.BlockSpec((tm, k), lambda j, i: (i, 0)),
            pl.BlockSpec((None, k, tn), lambda j, i: (layer, 0, j)),
            pl.BlockSpec((tm, tn), lambda j, i: (i, j)),
            pl.BlockSpec((None, None, 1, tn), lambda j, i: (i // per_b, j_gate, 0, j)),
        ],
        out_specs=pl.BlockSpec((tm, tn), lambda j, i: (i, j)),
        out_shape=jax.ShapeDtypeStruct((t, n), F32),
        scratch_shapes=[pltpu.VMEM((k, tn), BF16)],
        compiler_params=_params("parallel", "arbitrary"),
        name=name,
    )(a, w, x2, mod)


def _ffn_up_kernel(h_ref, wg_ref, wu_ref, wc_ref, o_ref, tail_scr, wg_scr, wu_scr, *, tiles_per_seq):
    tm = h_ref.shape[0]
    i = pl.program_id(1)

    @pl.when(i == 0)
    def _():
        wg_scr[...] = wg_ref[...].astype(BF16)
        wu_scr[...] = wu_ref[...].astype(BF16)

    @pl.when(i % tiles_per_seq == 0)
    def _():
        tail_scr[...] = jnp.zeros_like(tail_scr)

    h = h_ref[...]
    gate = _dot(h, wg_scr[...])
    up = _dot(h, wu_scr[...])
    wc = wc_ref[...]
    xp = jnp.concatenate([tail_scr[...], gate], axis=0)
    conv = gate * wc[FF_CONV - 1:FF_CONV]
    for back in range(1, FF_CONV):
        conv = conv + pltpu.roll(xp, back, 0)[CONV_HALO:] * wc[FF_CONV - 1 - back:FF_CONV - back]
    tail_scr[...] = gate[tm - CONV_HALO:]
    o_ref[...] = (_silu(conv) * up).astype(o_ref.dtype)


def _ffn_up(h2, w_up, w_ffconv, layer, seq):
    t, k = h2.shape
    d_ff = w_ffconv.shape[2]
    tm = min(1024, seq)
    tn = 512
    nj = d_ff // tn
    return pl.pallas_call(
        functools.partial(_ffn_up_kernel, tiles_per_seq=seq // tm),
        grid=(nj, t // tm),
        in_specs=[
            pl.BlockSpec((tm, k), lambda j, i: (i, 0)),
            pl.BlockSpec((None, k, tn), lambda j, i: (layer, 0, j)),
            pl.BlockSpec((None, k, tn), lambda j, i: (layer, 0, nj + j)),
            pl.BlockSpec((None, FF_CONV, tn), lambda j, i: (layer, 0, j)),
        ],
        out_specs=pl.BlockSpec((tm, tn), lambda j, i: (i, j)),
        out_shape=jax.ShapeDtypeStruct((t, d_ff), BF16),
        scratch_shapes=[pltpu.VMEM((CONV_HALO, tn), F32), pltpu.VMEM((k, tn), BF16),
                        pltpu.VMEM((k, tn), BF16)],
        compiler_params=_params("parallel", "arbitrary"),
        name="ffn_up",
    )(h2, w_up, w_up, w_ffconv)


def kernel(x, c, w_ada, b_ada, g_mix, g_ffn, w_in, b_gate_if, w_mconv, g_mout, g_dq, g_dk,
           lam_q1, lam_k1, lam_q2, lam_k2, g_dsub, w_branch, w_out, w_up, w_ffconv, w_down):
    batch, seq, d = x.shape
    depth = w_ada.shape[0]
    width = d // 2
    t = batch * seq
    n_gate = 2 * M_HEADS

    mod_all = _ada_mod(c, w_ada, b_ada)
    x2 = x.reshape(t, d)
    w_in_t = jnp.swapaxes(w_in, 1, 2)

    for l in range(depth):
        mod = mod_all[l]
        gate0 = 4 * width
        sb_scale = (width // SB_HEADS) ** -0.5 * LOG2E
        b_if = jnp.zeros((1, LANE), F32).at[0, :n_gate].set(b_gate_if[l])

        h = _norm_mod(x2, g_mix[l], mod, 1, 0, seq)
        zmain = _in_proj(h, w_in_t, l, width, n_gate, 7, sb_scale)
        zif = _gate_proj(h, w_in_t, l, gate0, b_if, n_gate)

        out_m = _mlstm(zmain, zif, w_mconv[l], g_mout[l], batch, seq, width)
        qn, kn = _qk_norm(zmain, g_dq[l], g_dk[l], width, 4, 5, seq)
        lam_init = 0.8 - 0.6 * math.exp(-0.3 * l)
        out_d = _diff_attn(qn, kn, zmain, 6, (lam_q1[l], lam_k1[l], lam_q2[l], lam_k2[l]),
                           g_dsub[l], lam_init, batch, seq, width)
        out_s = _stick_breaking(zmain, 7, 8, 9, batch, seq, width)

        merged = _merge(out_m, out_d, out_s, w_branch, l, zmain, 10 * width, d)
        x2 = _proj_residual(merged, w_out, l, x2, mod, 2, seq, 1024, 1024, "out_proj")

        h2 = _norm_mod(x2, g_ffn[l], mod, 4, 3, seq)
        act = _ffn_up(h2, w_up, w_ffconv, l, seq)
        x2 = _proj_residual(act, w_down, l, x2, mod, 5, seq, 512, 512, "ffn_down")

    return x2.reshape(batch, seq, d)
```

```python
import functools
import math

import jax
import jax.numpy as jnp
from jax import lax
from jax.experimental import pallas as pl
from jax.experimental.pallas import tpu as pltpu

EPS = 1e-6
M_HEADS = 4
M_CONV = 4
M_CHUNK = 128
DA_HEADS = 8
SB_HEADS = 8
N_MOD = 6
FF_CONV = 3
LANE = 128
CONV_HALO = 8
BF16_SUBLANES = 16
VMEM_LIMIT_BYTES = 56 * 1024 * 1024
HIGHEST = lax.Precision.HIGHEST
NEG_INF = float("-inf")
LOG2E = math.log2(math.e)
STICK_DEAD_LOG2 = 152.0

BF16 = jnp.bfloat16
F32 = jnp.float32


def _params(*sem):
    return pltpu.CompilerParams(dimension_semantics=sem, vmem_limit_bytes=VMEM_LIMIT_BYTES)


def _dot(a, b):
    return jnp.dot(a, b, preferred_element_type=F32)


def _dot_nt(a, b):
    return lax.dot_general(a, b, (((1,), (1,)), ((), ())), preferred_element_type=F32)


def _dot_tn(a, b):
    return lax.dot_general(a, b, (((0,), (0,)), ((), ())), preferred_element_type=F32)


def _silu(x):
    return x * jax.nn.sigmoid(x)


def _softplus(x):
    return jnp.maximum(x, 0.0) + jnp.log1p(jnp.exp(-jnp.abs(x)))


def _log_sigmoid(x):
    return -_softplus(-x)


def _ada_kernel(c_ref, w_ref, b_ref, o_ref):
    c = c_ref[...]
    o_ref[...] = jnp.dot(_silu(c), w_ref[...], preferred_element_type=F32,
                         precision=HIGHEST) + b_ref[...]


def _ada_mod(c, w_ada, b_ada):
    n_layers, d, n = w_ada.shape
    b = c.shape[0]
    rows = 8
    c_pad = jnp.zeros((rows, d), F32).at[:b].set(c)
    tn = 1024
    out = pl.pallas_call(
        _ada_kernel,
        grid=(n_layers, n // tn),
        in_specs=[
            pl.BlockSpec((rows, d), lambda l, j: (0, 0)),
            pl.BlockSpec((None, d, tn), lambda l, j: (l, 0, j)),
            pl.BlockSpec((None, 1, tn), lambda l, j: (l, 0, j)),
        ],
        out_specs=pl.BlockSpec((None, rows, tn), lambda l, j: (l, 0, j)),
        out_shape=jax.ShapeDtypeStruct((n_layers, rows, n), F32),
        compiler_params=_params("parallel", "parallel"),
        name="ada_mod",
    )(c_pad, w_ada, b_ada.reshape(n_layers, 1, n))
    return out[:, :b].reshape(n_layers, b, N_MOD, 1, d)


def _norm_mod_kernel(x_ref, g_ref, sc_ref, sh_ref, o_ref):
    x = x_ref[...]
    y = x * lax.rsqrt(jnp.mean(x * x, axis=-1, keepdims=True) + EPS) * g_ref[...]
    o_ref[...] = (y * (1.0 + sc_ref[...]) + sh_ref[...]).astype(o_ref.dtype)


def _norm_mod(x2, g, mod, j_scale, j_shift, seq):
    t, d = x2.shape
    ts = min(512, seq)
    per_b = seq // ts
    return pl.pallas_call(
        _norm_mod_kernel,
        grid=(t // ts,),
        in_specs=[
            pl.BlockSpec((ts, d), lambda i: (i, 0)),
            pl.BlockSpec((1, d), lambda i: (0, 0)),
            pl.BlockSpec((None, None, 1, d), lambda i: (i // per_b, j_scale, 0, 0)),
            pl.BlockSpec((None, None, 1, d), lambda i: (i // per_b, j_shift, 0, 0)),
        ],
        out_specs=pl.BlockSpec((ts, d), lambda i: (i, 0)),
        out_shape=jax.ShapeDtypeStruct((t, d), BF16),
        compiler_params=_params("parallel"),
        name="norm_mod",
    )(x2, g.reshape(1, d), mod, mod)


def _in_proj_kernel(h_ref, wt_ref, o_ref, w_scr, *, scaled_block, scale):
    m = pl.program_id(0)

    @pl.when(pl.program_id(1) == 0)
    def _():
        w = wt_ref[0] * jnp.where(m == scaled_block, scale, 1.0)
        w_scr[...] = w.T.astype(BF16)

    o_ref[...] = _dot(h_ref[...], w_scr[...]).astype(o_ref.dtype)


def _in_proj(h, w_in_t, layer, width, n_gate, scaled_block, scale):
    t, k = h.shape
    n_out = w_in_t.shape[1] - n_gate
    tm = min(1024, t)
    tn = width
    aligned_blocks = 4
    row0 = lambda m: pl.multiple_of(m * tn + jnp.where(m >= aligned_blocks, n_gate, 0), n_gate)
    return pl.pallas_call(
        functools.partial(_in_proj_kernel, scaled_block=scaled_block, scale=scale),
        grid=(n_out // tn, t // tm),
        in_specs=[
            pl.BlockSpec((tm, k), lambda m, i: (i, 0)),
            pl.BlockSpec((pl.Element(1), pl.Element(tn), pl.Element(k)),
                         lambda m, i: (layer, row0(m), 0)),
        ],
        out_specs=pl.BlockSpec((tm, tn), lambda m, i: (i, m)),
        out_shape=jax.ShapeDtypeStruct((t, n_out), BF16),
        scratch_shapes=[pltpu.VMEM((k, tn), BF16)],
        compiler_params=_params("parallel", "arbitrary"),
        name="in_proj",
    )(h, w_in_t)


def _gate_proj_kernel(a_ref, wt_ref, b_ref, o_ref):
    wt = wt_ref[...]
    pad = jnp.zeros((o_ref.shape[1] - wt.shape[0], wt.shape[1]), wt.dtype)
    w = jnp.concatenate([wt, pad], axis=0).astype(BF16)
    o_ref[...] = _dot_nt(a_ref[...], w) + b_ref[...]


def _gate_proj(a, w_in_t, layer, gate0, bias, n_gate):
    t, k = a.shape
    tm = min(1024, t)
    return pl.pallas_call(
        _gate_proj_kernel,
        grid=(t // tm,),
        in_specs=[
            pl.BlockSpec((tm, k), lambda i: (i, 0)),
            pl.BlockSpec((None, n_gate, k), lambda i: (layer, gate0 // n_gate, 0)),
            pl.BlockSpec((1, LANE), lambda i: (0, 0)),
        ],
        out_specs=pl.BlockSpec((tm, LANE), lambda i: (i, 0)),
        out_shape=jax.ShapeDtypeStruct((t, LANE), F32),
        compiler_params=_params("parallel"),
        name="gate_proj",
    )(a, w_in_t, bias)


def _mlstm_kernel(q_ref, k_ref, v_ref, o_ref, zif_ref, wconv_ref, g_ref, out_ref,
                  c_scr, n_scr, m_scr, qtail_scr, ktail_scr, shift_scr, *, dh):
    chunk = q_ref.shape[0]
    width = q_ref.shape[1]
    heads = width // dh
    halo = qtail_scr.shape[0]

    @pl.when(pl.program_id(1) == 0)
    def _():
        c_scr[...] = jnp.zeros_like(c_scr)
        n_scr[...] = jnp.zeros_like(n_scr)
        m_scr[...] = jnp.zeros_like(m_scr)
        qtail_scr[...] = jnp.zeros_like(qtail_scr)
        ktail_scr[...] = jnp.zeros_like(ktail_scr)
        srow = lax.broadcasted_iota(jnp.int32, shift_scr.shape, 0)
        scol = lax.broadcasted_iota(jnp.int32, shift_scr.shape, 1)
        delay = srow >> (chunk.bit_length() - 1)
        shift_scr[...] = (scol - halo == (srow & (chunk - 1)) - delay).astype(BF16)

    wconv = wconv_ref[...]
    shift = shift_scr[...]

    def conv_silu(x_ref, tail_scr, w):
        raw = x_ref[...]
        delayed = _dot(shift, jnp.concatenate([tail_scr[...], raw], axis=0))
        tail_scr[...] = raw[chunk - halo:]
        acc = delayed[:chunk] * w[M_CONV - 1:M_CONV]
        for back in range(1, M_CONV):
            acc = acc + delayed[back * chunk:(back + 1) * chunk] * w[M_CONV - 1 - back:M_CONV - back]
        return _silu(acc)

    q_all = conv_silu(q_ref, qtail_scr, wconv[:, :width]) * (dh ** -0.5)
    k_all = conv_silu(k_ref, ktail_scr, wconv[:, width:])

    row = lax.broadcasted_iota(jnp.int32, (chunk, chunk), 0)
    col = lax.broadcasted_iota(jnp.int32, (chunk, chunk), 1)
    causal = col <= row

    zif = zif_ref[...]
    lsf = _log_sigmoid(zif)
    p1 = lsf.astype(BF16)
    r1 = lsf - p1.astype(F32)
    p2 = r1.astype(BF16)
    p3 = (r1 - p2.astype(F32)).astype(BF16)
    tri = causal.astype(BF16)
    b_all = _dot(jnp.concatenate([tri, tri, tri], axis=1), jnp.concatenate([p1, p2, p3], axis=0))
    b_all_t = b_all.T
    zif_t = zif.T

    for h in range(heads):
        sl = slice(h * dh, (h + 1) * dh)
        qh = q_all[:, sl]
        kh = k_all[:, sl]
        qb = qh.astype(BF16)
        kb = kh.astype(BF16)
        vb = v_ref[:, sl]

        i_col = zif[:, h:h + 1]
        i_row = zif_t[h:h + 1, :]
        b_col = b_all[:, heads + h:heads + h + 1]
        b_row = b_all_t[heads + h:heads + h + 1, :]
        b_last = b_row[:, chunk - 1:chunk]

        m_prev = m_scr[h][:, :1]
        dmat = jnp.where(causal, b_col - b_row + i_row, NEG_INF)
        m_inter = b_col + m_prev
        m_t = jnp.maximum(jnp.max(dmat, axis=-1, keepdims=True), m_inter)
        w = _dot_nt(qb, kb) * jnp.exp(dmat - m_t)
        decay = jnp.exp(m_inter - m_t)
        c_prev = c_scr[h]
        n_prev = n_scr[h]
        num = _dot(w.astype(BF16), vb) + decay * _dot(qb, c_prev.astype(BF16))
        den = jnp.sum(w, axis=-1, keepdims=True) + decay * jnp.sum(qh * n_prev, axis=-1, keepdims=True)
        hh = num / jnp.maximum(jnp.abs(den), jnp.exp(-m_t))

        hn = hh * lax.rsqrt(jnp.mean(hh * hh, axis=-1, keepdims=True) + EPS) * g_ref[:, sl]
        out_ref[:, sl] = (hn * jax.nn.sigmoid(o_ref[:, sl].astype(F32))).astype(out_ref.dtype)

        g_col = b_last - b_col + i_col
        m_new = jnp.maximum(b_last + m_prev, jnp.max(g_col, axis=0, keepdims=True))
        wk = jnp.exp(g_col - m_new)
        carry_decay = jnp.exp(b_last + m_prev - m_new)
        kw = kh * wk
        c_scr[h] = carry_decay * c_prev + _dot_tn(kw.astype(BF16), vb)
        n_scr[h] = carry_decay * n_prev + jnp.sum(kw, axis=0, keepdims=True)
        m_scr[h] = jnp.broadcast_to(m_new, m_scr.shape[1:])


def _mlstm(zmain, zif, w_mconv, g_mout, batch, seq, width):
    t = zmain.shape[0]
    dh = width // M_HEADS
    nc = seq // M_CHUNK
    row_map = lambda cb: (lambda b, c: (b * nc + c, cb))
    return pl.pallas_call(
        functools.partial(_mlstm_kernel, dh=dh),
        grid=(batch, nc),
        in_specs=[
            pl.BlockSpec((M_CHUNK, width), row_map(0)),
            pl.BlockSpec((M_CHUNK, width), row_map(1)),
            pl.BlockSpec((M_CHUNK, width), row_map(2)),
            pl.BlockSpec((M_CHUNK, width), row_map(3)),
            pl.BlockSpec((M_CHUNK, LANE), row_map(0)),
            pl.BlockSpec((M_CONV, 2 * width), lambda b, c: (0, 0)),
            pl.BlockSpec((1, width), lambda b, c: (0, 0)),
        ],
        out_specs=pl.BlockSpec((M_CHUNK, width), row_map(0)),
        out_shape=jax.ShapeDtypeStruct((t, width), BF16),
        scratch_shapes=[
            pltpu.VMEM((M_HEADS, dh, dh), F32),
            pltpu.VMEM((M_HEADS, 1, dh), F32),
            pltpu.VMEM((M_HEADS, 1, LANE), F32),
            pltpu.VMEM((BF16_SUBLANES, width), BF16),
            pltpu.VMEM((BF16_SUBLANES, width), BF16),
            pltpu.VMEM((M_CONV * M_CHUNK, BF16_SUBLANES + M_CHUNK), BF16),
        ],
        compiler_params=_params("parallel", "arbitrary"),
        name="mlstm",
    )(zmain, zmain, zmain, zmain, zif, w_mconv, g_mout.reshape(1, width))


def _qk_norm_kernel(q_ref, k_ref, gq_ref, gk_ref, qo_ref, ko_ref, *, dh):
    width = q_ref.shape[1]
    row = lax.broadcasted_iota(jnp.int32, (LANE, LANE), 0)
    col = lax.broadcasted_iota(jnp.int32, (LANE, LANE), 1)
    same_group = ((row // dh) == (col // dh)).astype(BF16)

    def group_rms(x_ref, g_ref, o_ref, scale):
        for tile in range(width // LANE):
            sl = slice(tile * LANE, (tile + 1) * LANE)
            x = x_ref[:, sl].astype(F32)
            sq = x * x
            hi = sq.astype(BF16)
            lo = (sq - hi.astype(F32)).astype(BF16)
            ssum = _dot(hi, same_group) + _dot(lo, same_group)
            y = x * lax.rsqrt(ssum * (1.0 / dh) + EPS) * g_ref[:, sl]
            o_ref[:, sl] = (y * scale).astype(o_ref.dtype)

    group_rms(q_ref, gq_ref, qo_ref, dh ** -0.5 * LOG2E)
    group_rms(k_ref, gk_ref, ko_ref, 1.0)


def _qk_norm(zmain, g_dq, g_dk, width, col_q, col_k, seq):
    t = zmain.shape[0]
    dh = g_dq.shape[0]
    ts = min(512, seq)
    reps = width // dh
    return pl.pallas_call(
        functools.partial(_qk_norm_kernel, dh=dh),
        grid=(t // ts,),
        in_specs=[
            pl.BlockSpec((ts, width), lambda i: (i, col_q)),
            pl.BlockSpec((ts, width), lambda i: (i, col_k)),
            pl.BlockSpec((1, width), lambda i: (0, 0)),
            pl.BlockSpec((1, width), lambda i: (0, 0)),
        ],
        out_specs=[pl.BlockSpec((ts, width), lambda i: (i, 0)),
                   pl.BlockSpec((ts, width), lambda i: (i, 0))],
        out_shape=[jax.ShapeDtypeStruct((t, width), BF16), jax.ShapeDtypeStruct((t, width), BF16)],
        compiler_params=_params("parallel"),
        name="qk_norm",
    )(zmain, zmain, jnp.tile(g_dq, reps).reshape(1, width), jnp.tile(g_dk, reps).reshape(1, width))


def _diff_attn_kernel(q_ref, k_ref, v_ref, lq1_ref, lk1_ref, lq2_ref, lk2_ref, g_ref, out_ref,
                      m_scr, l_scr, acc_scr, s_scr, *, lam_init, dh):
    tq = q_ref.shape[0]
    i = pl.program_id(2)
    q = q_ref[...]
    lane = lax.broadcasted_iota(jnp.int32, q.shape, 1)
    zero = jnp.zeros_like(q)
    q_halves = (jnp.where(lane < dh, q, zero), jnp.where(lane >= dh, q, zero))

    m_scr[...] = jnp.full_like(m_scr, NEG_INF)
    l_scr[...] = jnp.zeros_like(l_scr)
    acc_scr[...] = jnp.zeros_like(acc_scr)

    def key_block(ref, j):
        return ref[pl.ds(pl.multiple_of(j * tq, tq), tq), :]

    def put_scores(j, slot):
        k = key_block(k_ref, j)
        for c in range(2):
            s_scr[slot, c] = _dot_nt(k, q_halves[c])

    def consume(j, slot, masked):
        v = key_block(v_ref, j)
        probs, alphas = [], []
        for c in range(2):
            s = s_scr[slot, c]
            if masked:
                key = lax.broadcasted_iota(jnp.int32, s.shape, 0)
                qry = lax.broadcasted_iota(jnp.int32, s.shape, 1)
                s = jnp.where(key <= qry, s, NEG_INF)
            m_prev = m_scr[c]
            m_new = jnp.maximum(m_prev, jnp.max(s, axis=0, keepdims=True))
            alpha = jnp.exp2(m_prev - m_new)
            p = jnp.exp2(s - m_new)
            l_scr[c] = alpha * l_scr[c] + jnp.sum(p, axis=0, keepdims=True)
            m_scr[c] = m_new
            probs.append(p.astype(BF16))
            alphas.append(alpha)
        for c in range(2):
            acc_scr[c] = alphas[c] * acc_scr[c] + _dot_tn(v, probs[c])

    put_scores(0, 0)

    def pair(p, carry):
        j = 2 * p
        put_scores(j + 1, 1)
        consume(j, 0, False)
        put_scores(j + 2, 0)
        consume(j + 1, 1, False)
        return carry

    lax.fori_loop(0, i // 2, pair, 0)

    @pl.when(i % 2 == 0)
    def _():
        consume(i, 0, True)

    @pl.when(i % 2 == 1)
    def _():
        put_scores(i, 1)
        consume(i - 1, 0, False)
        consume(i, 1, True)

    lam = (jnp.exp(jnp.sum(lq1_ref[...] * lk1_ref[...], axis=-1, keepdims=True))
           - jnp.exp(jnp.sum(lq2_ref[...] * lk2_ref[...], axis=-1, keepdims=True)) + lam_init)
    o = (acc_scr[0] / l_scr[0] - lam * (acc_scr[1] / l_scr[1])).T
    on = o * lax.rsqrt(jnp.mean(o * o, axis=-1, keepdims=True) + EPS) * g_ref[...]
    out_ref[...] = (on * (1.0 - lam_init)).astype(out_ref.dtype)


def _diff_attn(qn, kn, zmain, col_v, lam_params, g_dsub, lam_init, batch, seq, width):
    t = qn.shape[0]
    dh = lam_params[0].shape[0]
    hw = 2 * dh
    heads = width // hw
    tq = min(512, seq)
    nq = seq // tq
    v_col0 = col_v * (width // hw)
    lam_specs = [pl.BlockSpec((1, dh), lambda b, h, i: (0, 0)) for _ in range(4)]
    return pl.pallas_call(
        functools.partial(_diff_attn_kernel, lam_init=lam_init, dh=dh),
        grid=(batch, heads, nq),
        in_specs=[
            pl.BlockSpec((tq, hw), lambda b, h, i: (b * nq + i, h)),
            pl.BlockSpec((seq, hw), lambda b, h, i: (b, h)),
            pl.BlockSpec((seq, hw), lambda b, h, i: (b, v_col0 + h)),
            *lam_specs,
            pl.BlockSpec((1, hw), lambda b, h, i: (0, 0)),
        ],
        out_specs=pl.BlockSpec((tq, hw), lambda b, h, i: (b * nq + i, h)),
        out_shape=jax.ShapeDtypeStruct((t, width), BF16),
        scratch_shapes=[
            pltpu.VMEM((2, 1, tq), F32),
            pltpu.VMEM((2, 1, tq), F32),
            pltpu.VMEM((2, hw, tq), F32),
            pltpu.VMEM((2, 2, tq, tq), F32),
        ],
        compiler_params=_params("parallel", "parallel", "arbitrary"),
        name="diff_attn",
    )(qn, kn, zmain, *[p.reshape(1, dh) for p in lam_params], g_dsub.reshape(1, hw))


def _stick_kernel(q_ref, k_ref, v_ref, out_ref, carry_scr, acc_scr, *, sub, far):
    tq, dh = q_ref.shape
    i = pl.program_id(2)
    sign_bit = jnp.uint32(1 << 31)

    srow = lax.broadcasted_iota(jnp.int32, (sub, 2 * sub), 0)
    jcol = lax.broadcasted_iota(jnp.int32, (sub, 2 * sub), 1)
    later = ((jcol & (sub - 1)) > srow).astype(BF16)

    q = q_ref[...]

    def softplus2(z2):
        neg_abs = pltpu.bitcast(pltpu.bitcast(z2, jnp.uint32) | sign_bit, F32)
        return jnp.maximum(z2, 0.0) + jnp.log(1.0 + jnp.exp2(neg_abs)) * LOG2E

    def suffix_in_block(sp2):
        hi = sp2.astype(BF16)
        lo = (sp2 - hi.astype(F32)).astype(BF16)
        return _dot(later, jnp.concatenate([hi, lo], axis=0))

    start = pl.multiple_of(i * tq, tq)
    z_tile = _dot_nt(k_ref[pl.ds(start, tq), :], q)
    carry = jnp.zeros((1, tq), F32)
    key_rows = []
    for c in reversed(range(tq // sub)):
        lane0 = c * sub
        z2 = z_tile[lane0:lane0 + sub, lane0:]
        before = (lax.broadcasted_iota(jnp.int32, z2.shape, 0)
                  < lax.broadcasted_iota(jnp.int32, z2.shape, 1))
        sp2 = jnp.where(before, softplus2(z2), 0.0)
        a = jnp.exp2(z2 - sp2 - (suffix_in_block(sp2) + carry[:, lane0:]))
        a = jnp.where(before, a, 0.0).astype(BF16)
        total = jnp.sum(sp2, axis=0, keepdims=True)
        if lane0:
            a = jnp.concatenate([jnp.zeros((sub, lane0), BF16), a], axis=1)
            total = jnp.concatenate([jnp.zeros((1, lane0), F32), total], axis=1)
        key_rows.insert(0, a)
        carry = carry + total
    acc = _dot_tn(v_ref[pl.ds(start, tq), :], jnp.concatenate(key_rows, axis=0))

    def far_keys(jb, carry, valid):
        start = pl.multiple_of(jb * far, far)
        z2 = _dot_nt(k_ref[pl.ds(start, far), :], q)
        sp2 = softplus2(z2)
        if valid is not None:
            sp2 = jnp.where(valid, sp2, 0.0)
        parts = []
        for c in reversed(range(far // sub)):
            rows = slice(c * sub, (c + 1) * sub)
            a = jnp.exp2(z2[rows] - sp2[rows] - (suffix_in_block(sp2[rows]) + carry))
            if valid is not None:
                a = jnp.where(valid, a, 0.0)
            parts.insert(0, a.astype(BF16))
            carry = carry + jnp.sum(sp2[rows], axis=0, keepdims=True)
        return _dot_tn(v_ref[pl.ds(start, far), :], jnp.concatenate(parts, axis=0)), carry

    jb = i * (tq // far) - 1
    near, carry = far_keys(jnp.maximum(jb, 0), carry, i > 0)
    acc_scr[...] = acc + near
    carry_scr[...] = carry

    def keep_going(state):
        jb, min_carry = state
        return jnp.logical_and(jb >= 0, min_carry <= STICK_DEAD_LOG2)

    def body(state):
        jb, _ = state
        more, carry = far_keys(jb, carry_scr[...], None)
        acc_scr[...] += more
        carry_scr[...] = carry
        return jb - 1, jnp.min(carry)

    lax.while_loop(keep_going, body, (jb - 1, jnp.min(carry)))
    out_ref[...] = acc_scr[...].T.astype(out_ref.dtype)


def _stick_breaking(zmain, col_q, col_k, col_v, batch, seq, width):
    t = zmain.shape[0]
    dh = width // SB_HEADS
    tq = min(512, seq)
    nq = seq // tq
    per = width // dh
    return pl.pallas_call(
        functools.partial(_stick_kernel, sub=LANE, far=min(256, seq)),
        grid=(batch, SB_HEADS, nq),
        in_specs=[
            pl.BlockSpec((tq, dh), lambda b, h, i: (b * nq + i, col_q * per + h)),
            pl.BlockSpec((seq, dh), lambda b, h, i: (b, col_k * per + h)),
            pl.BlockSpec((seq, dh), lambda b, h, i: (b, col_v * per + h)),
        ],
        out_specs=pl.BlockSpec((tq, dh), lambda b, h, i: (b * nq + i, h)),
        out_shape=jax.ShapeDtypeStruct((t, width), BF16),
        scratch_shapes=[
            pltpu.VMEM((1, tq), F32),
            pltpu.VMEM((dh, tq), F32),
        ],
        compiler_params=_params("parallel", "parallel", "arbitrary"),
        name="stick_breaking",
    )(zmain, zmain, zmain)


def _merge_kernel(om_ref, od_ref, os_ref, wm_ref, wd_ref, ws_ref, gm_ref, gd_ref, gs_ref, o_ref, w_scr):
    @pl.when(pl.program_id(1) == 0)
    def _():
        for br, w_ref in enumerate((wm_ref, wd_ref, ws_ref)):
            w_scr[br] = w_ref[...].astype(BF16)

    acc = jax.nn.sigmoid(gm_ref[...].astype(F32)) * _dot(om_ref[...], w_scr[0])
    acc = acc + jax.nn.sigmoid(gd_ref[...].astype(F32)) * _dot(od_ref[...], w_scr[1])
    acc = acc + jax.nn.sigmoid(gs_ref[...].astype(F32)) * _dot(os_ref[...], w_scr[2])
    o_ref[...] = acc.astype(o_ref.dtype)


def _merge(out_m, out_d, out_s, w_branch, layer, zmain, gate_col0, d_model):
    t, width = out_m.shape
    tm = min(1024, t)
    tn = 512
    nj = d_model // tn
    act_spec = pl.BlockSpec((tm, width), lambda j, i: (i, 0))
    w_spec = lambda br: pl.BlockSpec((None, None, width, tn), lambda j, i: (layer, br, 0, j))
    gate_blk0 = gate_col0 // tn
    gate_spec = lambda br: pl.BlockSpec((tm, tn), lambda j, i: (i, gate_blk0 + br * nj + j))
    return pl.pallas_call(
        _merge_kernel,
        grid=(nj, t // tm),
        in_specs=[act_spec, act_spec, act_spec, w_spec(0), w_spec(1), w_spec(2),
                  gate_spec(0), gate_spec(1), gate_spec(2)],
        out_specs=pl.BlockSpec((tm, tn), lambda j, i: (i, j)),
        out_shape=jax.ShapeDtypeStruct((t, d_model), BF16),
        scratch_shapes=[pltpu.VMEM((3, width, tn), BF16)],
        compiler_params=_params("parallel", "arbitrary"),
        name="branch_merge",
    )(out_m, out_d, out_s, w_branch, w_branch, w_branch, zmain, zmain, zmain)


def _proj_residual_kernel(a_ref, w_ref, x_ref, gt_ref, o_ref, w_scr):
    @pl.when(pl.program_id(1) == 0)
    def _():
        w_scr[...] = w_ref[...].astype(BF16)

    o_ref[...] = x_ref[...] + gt_ref[...] * _dot(a_ref[...], w_scr[...])


def _proj_residual(a, w, layer, x2, mod, j_gate, seq, tm, tn, name):
    t, k = a.shape
    n = w.shape[2]
    tm = min(tm, seq)
    per_b = seq // tm
    return pl.pallas_call(
        _proj_residual_kernel,
        grid=(n // tn, t // tm),
        in_specs=[
            pl.BlockSpec((tm, k), lambda j, i: (i, 0)),
            pl.BlockSpec((None, k, tn), lambda j, i: (layer, 0, j)),
            pl.BlockSpec((tm, tn), lambda j, i: (i, j)),
            pl.BlockSpec((None, None, 1, tn), lambda j, i: (i // per_b, j_gate, 0, j)),
        ],
        out_specs=pl.BlockSpec((tm, tn), lambda j, i: (i, j)),
        out_shape=jax.ShapeDtypeStruct((t, n), F32),
        scratch_shapes=[pltpu.VMEM((k, tn), BF16)],
        compiler_params=_params("parallel", "arbitrary"),
        name=name,
    )(a, w, x2, mod)


def _ffn_up_kernel(h_ref, wg_ref, wu_ref, wc_ref, o_ref, tail_scr, wg_scr, wu_scr, *, tiles_per_seq):
    tm = h_ref.shape[0]
    i = pl.program_id(1)

    @pl.when(i == 0)
    def _():
        wg_scr[...] = wg_ref[...].astype(BF16)
        wu_scr[...] = wu_ref[...].astype(BF16)

    @pl.when(i % tiles_per_seq == 0)
    def _():
        tail_scr[...] = jnp.zeros_like(tail_scr)

    h = h_ref[...]
    gate = _dot(h, wg_scr[...])
    up = _dot(h, wu_scr[...])
    wc = wc_ref[...]
    xp = jnp.concatenate([tail_scr[...], gate], axis=0)
    conv = gate * wc[FF_CONV - 1:FF_CONV]
    for back in range(1, FF_CONV):
        conv = conv + pltpu.roll(xp, back, 0)[CONV_HALO:] * wc[FF_CONV - 1 - back:FF_CONV - back]
    tail_scr[...] = gate[tm - CONV_HALO:]
    o_ref[...] = (_silu(conv) * up).astype(o_ref.dtype)


def _ffn_up(h2, w_up, w_ffconv, layer, seq):
    t, k = h2.shape
    d_ff = w_ffconv.shape[2]
    tm = min(1024, seq)
    tn = 512
    nj = d_ff // tn
    return pl.pallas_call(
        functools.partial(_ffn_up_kernel, tiles_per_seq=seq // tm),
        grid=(nj, t // tm),
        in_specs=[
            pl.BlockSpec((tm, k), lambda j, i: (i, 0)),
            pl.BlockSpec((None, k, tn), lambda j, i: (layer, 0, j)),
            pl.BlockSpec((None, k, tn), lambda j, i: (layer, 0, nj + j)),
            pl.BlockSpec((None, FF_CONV, tn), lambda j, i: (layer, 0, j)),
        ],
        out_specs=pl.BlockSpec((tm, tn), lambda j, i: (i, j)),
        out_shape=jax.ShapeDtypeStruct((t, d_ff), BF16),
        scratch_shapes=[pltpu.VMEM((CONV_HALO, tn), F32), pltpu.VMEM((k, tn), BF16),
                        pltpu.VMEM((k, tn), BF16)],
        compiler_params=_params("parallel", "arbitrary"),
        name="ffn_up",
    )(h2, w_up, w_up, w_ffconv)


def kernel(x, c, w_ada, b_ada, g_mix, g_ffn, w_in, b_gate_if, w_mconv, g_mout, g_dq, g_dk,
           lam_q1, lam_k1, lam_q2, lam_k2, g_dsub, w_branch, w_out, w_up, w_ffconv, w_down):
    batch, seq, d = x.shape
    depth = w_ada.shape[0]
    width = d // 2
    t = batch * seq
    n_gate = 2 * M_HEADS

    mod_all = _ada_mod(c, w_ada, b_ada)
    x2 = x.reshape(t, d)
    w_in_t = jnp.swapaxes(w_in, 1, 2)

    for l in range(depth):
        mod = mod_all[l]
        gate0 = 4 * width
        sb_scale = (width // SB_HEADS) ** -0.5 * LOG2E
        b_if = jnp.zeros((1, LANE), F32).at[0, :n_gate].set(b_gate_if[l])

        h = _norm_mod(x2, g_mix[l], mod, 1, 0, seq)
        zmain = _in_proj(h, w_in_t, l, width, n_gate, 7, sb_scale)
        zif = _gate_proj(h, w_in_t, l, gate0, b_if, n_gate)

        out_m = _mlstm(zmain, zif, w_mconv[l], g_mout[l], batch, seq, width)
        qn, kn = _qk_norm(zmain, g_dq[l], g_dk[l], width, 4, 5, seq)
        lam_init = 0.8 - 0.6 * math.exp(-0.3 * l)
        out_d = _diff_attn(qn, kn, zmain, 6, (lam_q1[l], lam_k1[l], lam_q2[l], lam_k2[l]),
                           g_dsub[l], lam_init, batch, seq, width)
        out_s = _stick_breaking(zmain, 7, 8, 9, batch, seq, width)

        merged = _merge(out_m, out_d, out_s, w_branch, l, zmain, 10 * width, d)
        x2 = _proj_residual(merged, w_out, l, x2, mod, 2, seq, 1024, 1024, "out_proj")

        h2 = _norm_mod(x2, g_ffn[l], mod, 4, 3, seq)
        act = _ffn_up(h2, w_up, w_ffconv, l, seq)
        x2 = _proj_residual(act, w_down, l, x2, mod, 5, seq, 512, 512, "ffn_down")

    return x2.reshape(batch, seq, d)
```

```python
import functools
import math

import jax
import jax.numpy as jnp
from jax import lax
from jax.experimental import pallas as pl
from jax.experimental.pallas import tpu as pltpu

EPS = 1e-6
M_HEADS = 4
M_CONV = 4
M_CHUNK = 128
DA_HEADS = 8
SB_HEADS = 8
N_MOD = 6
FF_CONV = 3
LANE = 128
CONV_HALO = 8
BF16_SUBLANES = 16
VMEM_LIMIT_BYTES = 56 * 1024 * 1024
HIGHEST = lax.Precision.HIGHEST
NEG_INF = float("-inf")
LOG2E = math.log2(math.e)
STICK_DEAD_LOG2 = 152.0

BF16 = jnp.bfloat16
F32 = jnp.float32


def _params(*sem):
    return pltpu.CompilerParams(dimension_semantics=sem, vmem_limit_bytes=VMEM_LIMIT_BYTES)


def _dot(a, b):
    return jnp.dot(a, b, preferred_element_type=F32)


def _dot_nt(a, b):
    return lax.dot_general(a, b, (((1,), (1,)), ((), ())), preferred_element_type=F32)


def _dot_tn(a, b):
    return lax.dot_general(a, b, (((0,), (0,)), ((), ())), preferred_element_type=F32)


def _silu(x):
    return x * jax.nn.sigmoid(x)


def _softplus(x):
    return jnp.maximum(x, 0.0) + jnp.log1p(jnp.exp(-jnp.abs(x)))


def _log_sigmoid(x):
    return -_softplus(-x)


def _ada_kernel(c_ref, w_ref, b_ref, o_ref):
    c = c_ref[...]
    o_ref[...] = jnp.dot(_silu(c), w_ref[...], preferred_element_type=F32,
                         precision=HIGHEST) + b_ref[...]


def _ada_mod(c, w_ada, b_ada):
    n_layers, d, n = w_ada.shape
    b = c.shape[0]
    rows = 8
    c_pad = jnp.zeros((rows, d), F32).at[:b].set(c)
    tn = 1024
    out = pl.pallas_call(
        _ada_kernel,
        grid=(n_layers, n // tn),
        in_specs=[
            pl.BlockSpec((rows, d), lambda l, j: (0, 0)),
            pl.BlockSpec((None, d, tn), lambda l, j: (l, 0, j)),
            pl.BlockSpec((None, 1, tn), lambda l, j: (l, 0, j)),
        ],
        out_specs=pl.BlockSpec((None, rows, tn), lambda l, j: (l, 0, j)),
        out_shape=jax.ShapeDtypeStruct((n_layers, rows, n), F32),
        compiler_params=_params("parallel", "parallel"),
        name="ada_mod",
    )(c_pad, w_ada, b_ada.reshape(n_layers, 1, n))
    return out[:, :b].reshape(n_layers, b, N_MOD, 1, d)


def _norm_mod_kernel(x_ref, g_ref, sc_ref, sh_ref, o_ref):
    x = x_ref[...]
    y = x * lax.rsqrt(jnp.mean(x * x, axis=-1, keepdims=True) + EPS) * g_ref[...]
    o_ref[...] = (y * (1.0 + sc_ref[...]) + sh_ref[...]).astype(o_ref.dtype)


def _norm_mod(x2, g, mod, j_scale, j_shift, seq):
    t, d = x2.shape
    ts = min(512, seq)
    per_b = seq // ts
    return pl.pallas_call(
        _norm_mod_kernel,
        grid=(t // ts,),
        in_specs=[
            pl.BlockSpec((ts, d), lambda i: (i, 0)),
            pl.BlockSpec((1, d), lambda i: (0, 0)),
            pl.BlockSpec((None, None, 1, d), lambda i: (i // per_b, j_scale, 0, 0)),
            pl.BlockSpec((None, None, 1, d), lambda i: (i // per_b, j_shift, 0, 0)),
        ],
        out_specs=pl.BlockSpec((ts, d), lambda i: (i, 0)),
        out_shape=jax.ShapeDtypeStruct((t, d), BF16),
        compiler_params=_params("parallel"),
        name="norm_mod",
    )(x2, g.reshape(1, d), mod, mod)


def _in_proj_kernel(h_ref, wt_ref, o_ref, w_scr, *, scaled_block, scale):
    m = pl.program_id(0)

    @pl.when(pl.program_id(1) == 0)
    def _():
        w = wt_ref[0] * jnp.where(m == scaled_block, scale, 1.0)
        w_scr[...] = w.T.astype(BF16)

    o_ref[...] = _dot(h_ref[...], w_scr[...]).astype(o_ref.dtype)


def _in_proj(h, w_in_t, layer, width, n_gate, scaled_block, scale):
    t, k = h.shape
    n_out = w_in_t.shape[1] - n_gate
    tm = min(1024, t)
    tn = width
    aligned_blocks = 4
    row0 = lambda m: pl.multiple_of(m * tn + jnp.where(m >= aligned_blocks, n_gate, 0), n_gate)
    return pl.pallas_call(
        functools.partial(_in_proj_kernel, scaled_block=scaled_block, scale=scale),
        grid=(n_out // tn, t // tm),
        in_specs=[
            pl.BlockSpec((tm, k), lambda m, i: (i, 0)),
            pl.BlockSpec((pl.Element(1), pl.Element(tn), pl.Element(k)),
                         lambda m, i: (layer, row0(m), 0)),
        ],
        out_specs=pl.BlockSpec((tm, tn), lambda m, i: (i, m)),
        out_shape=jax.ShapeDtypeStruct((t, n_out), BF16),
        scratch_shapes=[pltpu.VMEM((k, tn), BF16)],
        compiler_params=_params("parallel", "arbitrary"),
        name="in_proj",
    )(h, w_in_t)


def _gate_proj_kernel(a_ref, wt_ref, b_ref, o_ref):
    wt = wt_ref[...]
    pad = jnp.zeros((o_ref.shape[1] - wt.shape[0], wt.shape[1]), wt.dtype)
    w = jnp.concatenate([wt, pad], axis=0).astype(BF16)
    o_ref[...] = _dot_nt(a_ref[...], w) + b_ref[...]


def _gate_proj(a, w_in_t, layer, gate0, bias, n_gate):
    t, k = a.shape
    tm = min(1024, t)
    return pl.pallas_call(
        _gate_proj_kernel,
        grid=(t // tm,),
        in_specs=[
            pl.BlockSpec((tm, k), lambda i: (i, 0)),
            pl.BlockSpec((None, n_gate, k), lambda i: (layer, gate0 // n_gate, 0)),
            pl.BlockSpec((1, LANE), lambda i: (0, 0)),
        ],
        out_specs=pl.BlockSpec((tm, LANE), lambda i: (i, 0)),
        out_shape=jax.ShapeDtypeStruct((t, LANE), F32),
        compiler_params=_params("parallel"),
        name="gate_proj",
    )(a, w_in_t, bias)


def _mlstm_kernel(q_ref, k_ref, v_ref, o_ref, zif_ref, wconv_ref, g_ref, out_ref,
                  c_scr, n_scr, m_scr, qtail_scr, ktail_scr, shift_scr, *, dh):
    chunk = q_ref.shape[0]
    width = q_ref.shape[1]
    heads = width // dh
    halo = qtail_scr.shape[0]

    @pl.when(pl.program_id(1) == 0)
    def _():
        c_scr[...] = jnp.zeros_like(c_scr)
        n_scr[...] = jnp.zeros_like(n_scr)
        m_scr[...] = jnp.zeros_like(m_scr)
        qtail_scr[...] = jnp.zeros_like(qtail_scr)
        ktail_scr[...] = jnp.zeros_like(ktail_scr)
        srow = lax.broadcasted_iota(jnp.int32, shift_scr.shape, 0)
        scol = lax.broadcasted_iota(jnp.int32, shift_scr.shape, 1)
        delay = srow >> (chunk.bit_length() - 1)
        shift_scr[...] = (scol - halo == (srow & (chunk - 1)) - delay).astype(BF16)

    wconv = wconv_ref[...]
    shift = shift_scr[...]

    def conv_silu(x_ref, tail_scr, w):
        raw = x_ref[...]
        delayed = _dot(shift, jnp.concatenate([tail_scr[...], raw], axis=0))
        tail_scr[...] = raw[chunk - halo:]
        acc = delayed[:chunk] * w[M_CONV - 1:M_CONV]
        for back in range(1, M_CONV):
            acc = acc + delayed[back * chunk:(back + 1) * chunk] * w[M_CONV - 1 - back:M_CONV - back]
        return _silu(acc)

    q_all = conv_silu(q_ref, qtail_scr, wconv[:, :width]) * (dh ** -0.5)
    k_all = conv_silu(k_ref, ktail_scr, wconv[:, width:])

    row = lax.broadcasted_iota(jnp.int32, (chunk, chunk), 0)
    col = lax.broadcasted_iota(jnp.int32, (chunk, chunk), 1)
    causal = col <= row

    zif = zif_ref[...]
    lsf = _log_sigmoid(zif)
    p1 = lsf.astype(BF16)
    r1 = lsf - p1.astype(F32)
    p2 = r1.astype(BF16)
    p3 = (r1 - p2.astype(F32)).astype(BF16)
    tri = causal.astype(BF16)
    b_all = _dot(jnp.concatenate([tri, tri, tri], axis=1), jnp.concatenate([p1, p2, p3], axis=0))
    b_all_t = b_all.T
    zif_t = zif.T

    for h in range(heads):
        sl = slice(h * dh, (h + 1) * dh)
        qh = q_all[:, sl]
        kh = k_all[:, sl]
        qb = qh.astype(BF16)
        kb = kh.astype(BF16)
        vb = v_ref[:, sl]

        i_col = zif[:, h:h + 1]
        i_row = zif_t[h:h + 1, :]
        b_col = b_all[:, heads + h:heads + h + 1]
        b_row = b_all_t[heads + h:heads + h + 1, :]
        b_last = b_row[:, chunk - 1:chunk]

        m_prev = m_scr[h][:, :1]
        dmat = jnp.where(causal, b_col - b_row + i_row, NEG_INF)
        m_inter = b_col + m_prev
        m_t = jnp.maximum(jnp.max(dmat, axis=-1, keepdims=True), m_inter)
        w = _dot_nt(qb, kb) * jnp.exp(dmat - m_t)
        decay = jnp.exp(m_inter - m_t)
        c_prev = c_scr[h]
        n_prev = n_scr[h]
        num = _dot(w.astype(BF16), vb) + decay * _dot(qb, c_prev.astype(BF16))
        den = jnp.sum(w, axis=-1, keepdims=True) + decay * jnp.sum(qh * n_prev, axis=-1, keepdims=True)
        hh = num / jnp.maximum(jnp.abs(den), jnp.exp(-m_t))

        hn = hh * lax.rsqrt(jnp.mean(hh * hh, axis=-1, keepdims=True) + EPS) * g_ref[:, sl]
        out_ref[:, sl] = (hn * jax.nn.sigmoid(o_ref[:, sl].astype(F32))).astype(out_ref.dtype)

        g_col = b_last - b_col + i_col
        m_new = jnp.maximum(b_last + m_prev, jnp.max(g_col, axis=0, keepdims=True))
        wk = jnp.exp(g_col - m_new)
        carry_decay = jnp.exp(b_last + m_prev - m_new)
        kw = kh * wk
        c_scr[h] = carry_decay * c_prev + _dot_tn(kw.astype(BF16), vb)
        n_scr[h] = carry_decay * n_prev + jnp.sum(kw, axis=0, keepdims=True)
        m_scr[h] = jnp.broadcast_to(m_new, m_scr.shape[1:])


def _mlstm(zmain, zif, w_mconv, g_mout, batch, seq, width):
    t = zmain.shape[0]
    dh = width // M_HEADS
    nc = seq // M_CHUNK
    row_map = lambda cb: (lambda b, c: (b * nc + c, cb))
    return pl.pallas_call(
        functools.partial(_mlstm_kernel, dh=dh),
        grid=(batch, nc),
        in_specs=[
            pl.BlockSpec((M_CHUNK, width), row_map(0)),
            pl.BlockSpec((M_CHUNK, width), row_map(1)),
            pl.BlockSpec((M_CHUNK, width), row_map(2)),
            pl.BlockSpec((M_CHUNK, width), row_map(3)),
            pl.BlockSpec((M_CHUNK, LANE), row_map(0)),
            pl.BlockSpec((M_CONV, 2 * width), lambda b, c: (0, 0)),
            pl.BlockSpec((1, width), lambda b, c: (0, 0)),
        ],
        out_specs=pl.BlockSpec((M_CHUNK, width), row_map(0)),
        out_shape=jax.ShapeDtypeStruct((t, width), BF16),
        scratch_shapes=[
            pltpu.VMEM((M_HEADS, dh, dh), F32),
            pltpu.VMEM((M_HEADS, 1, dh), F32),
            pltpu.VMEM((M_HEADS, 1, LANE), F32),
            pltpu.VMEM((BF16_SUBLANES, width), BF16),
            pltpu.VMEM((BF16_SUBLANES, width), BF16),
            pltpu.VMEM((M_CONV * M_CHUNK, BF16_SUBLANES + M_CHUNK), BF16),
        ],
        compiler_params=_params("parallel", "arbitrary"),
        name="mlstm",
    )(zmain, zmain, zmain, zmain, zif, w_mconv, g_mout.reshape(1, width))


def _qk_norm_kernel(q_ref, k_ref, gq_ref, gk_ref, qo_ref, ko_ref, *, dh):
    width = q_ref.shape[1]
    row = lax.broadcasted_iota(jnp.int32, (LANE, LANE), 0)
    col = lax.broadcasted_iota(jnp.int32, (LANE, LANE), 1)
    same_group = ((row // dh) == (col // dh)).astype(BF16)

    def group_rms(x_ref, g_ref, o_ref, scale):
        for tile in range(width // LANE):
            sl = slice(tile * LANE, (tile + 1) * LANE)
            x = x_ref[:, sl].astype(F32)
            sq = x * x
            hi = sq.astype(BF16)
            lo = (sq - hi.astype(F32)).astype(BF16)
            ssum = _dot(hi, same_group) + _dot(lo, same_group)
            y = x * lax.rsqrt(ssum * (1.0 / dh) + EPS) * g_ref[:, sl]
            o_ref[:, sl] = (y * scale).astype(o_ref.dtype)

    group_rms(q_ref, gq_ref, qo_ref, dh ** -0.5 * LOG2E)
    group_rms(k_ref, gk_ref, ko_ref, 1.0)


def _qk_norm(zmain, g_dq, g_dk, width, col_q, col_k, seq):
    t = zmain.shape[0]
    dh = g_dq.shape[0]
    ts = min(512, seq)
    reps = width // dh
    return pl.pallas_call(
        functools.partial(_qk_norm_kernel, dh=dh),
        grid=(t // ts,),
        in_specs=[
            pl.BlockSpec((ts, width), lambda i: (i, col_q)),
            pl.BlockSpec((ts, width), lambda i: (i, col_k)),
            pl.BlockSpec((1, width), lambda i: (0, 0)),
            pl.BlockSpec((1, width), lambda i: (0, 0)),
        ],
        out_specs=[pl.BlockSpec((ts, width), lambda i: (i, 0)),
                   pl.BlockSpec((ts, width), lambda i: (i, 0))],
        out_shape=[jax.ShapeDtypeStruct((t, width), BF16), jax.ShapeDtypeStruct((t, width), BF16)],
        compiler_params=_params("parallel"),
        name="qk_norm",
    )(zmain, zmain, jnp.tile(g_dq, reps).reshape(1, width), jnp.tile(g_dk, reps).reshape(1, width))


def _diff_attn_kernel(q_ref, k_ref, v_ref, lq1_ref, lk1_ref, lq2_ref, lk2_ref, g_ref, out_ref,
                      m_scr, l_scr, acc_scr, s_scr, *, lam_init, dh):
    tq = q_ref.shape[0]
    hw = 2 * dh
    heads = q_ref.shape[1] // hw
    i = pl.program_id(2)
    head_lanes = [slice(h * hw, (h + 1) * hw) for h in range(heads)]

    q_streams = []
    for lanes in head_lanes:
        q = q_ref[:, lanes]
        lane = lax.broadcasted_iota(jnp.int32, q.shape, 1)
        zero = jnp.zeros_like(q)
        q_streams += [jnp.where(lane < dh, q, zero), jnp.where(lane >= dh, q, zero)]

    m_scr[...] = jnp.full_like(m_scr, NEG_INF)
    l_scr[...] = jnp.zeros_like(l_scr)
    acc_scr[...] = jnp.zeros_like(acc_scr)

    def key_block(ref, j, lanes):
        return ref[pl.ds(pl.multiple_of(j * tq, tq), tq), lanes]

    def put_scores(j, slot):
        for h, lanes in enumerate(head_lanes):
            k = key_block(k_ref, j, lanes)
            for c in range(2):
                s_scr[slot, 2 * h + c] = _dot_nt(k, q_streams[2 * h + c])

    def consume(j, slot, masked):
        probs, alphas = [], []
        for u in range(2 * heads):
            s = s_scr[slot, u]
            if masked:
                key = lax.broadcasted_iota(jnp.int32, s.shape, 0)
                qry = lax.broadcasted_iota(jnp.int32, s.shape, 1)
                s = jnp.where(key <= qry, s, NEG_INF)
            m_prev = m_scr[u]
            m_new = jnp.maximum(m_prev, jnp.max(s, axis=0, keepdims=True))
            alpha = jnp.exp2(m_prev - m_new)
            p = jnp.exp2(s - m_new)
            l_scr[u] = alpha * l_scr[u] + jnp.sum(p, axis=0, keepdims=True)
            m_scr[u] = m_new
            probs.append(p.astype(BF16))
            alphas.append(alpha)
        for h, lanes in enumerate(head_lanes):
            v = key_block(v_ref, j, lanes)
            for u in (2 * h, 2 * h + 1):
                acc_scr[u] = alphas[u] * acc_scr[u] + _dot_tn(v, probs[u])

    put_scores(0, 0)

    def pair(p, carry):
        j = 2 * p
        put_scores(j + 1, 1)
        consume(j, 0, False)
        put_scores(j + 2, 0)
        consume(j + 1, 1, False)
        return carry

    lax.fori_loop(0, i // 2, pair, 0)

    @pl.when(i % 2 == 0)
    def _():
        consume(i, 0, True)

    @pl.when(i % 2 == 1)
    def _():
        put_scores(i, 1)
        consume(i - 1, 0, False)
        consume(i, 1, True)

    lam = (jnp.exp(jnp.sum(lq1_ref[...] * lk1_ref[...], axis=-1, keepdims=True))
           - jnp.exp(jnp.sum(lq2_ref[...] * lk2_ref[...], axis=-1, keepdims=True)) + lam_init)
    for h, lanes in enumerate(head_lanes):
        u = 2 * h
        o = (acc_scr[u] / l_scr[u] - lam * (acc_scr[u + 1] / l_scr[u + 1])).T
        on = o * lax.rsqrt(jnp.mean(o * o, axis=-1, keepdims=True) + EPS) * g_ref[...]
        out_ref[:, lanes] = (on * (1.0 - lam_init)).astype(out_ref.dtype)


def _diff_attn(qn, kn, zmain, col_v, lam_params, g_dsub, lam_init, batch, seq, width):
    t = qn.shape[0]
    dh = lam_params[0].shape[0]
    hw = 2 * dh
    hb = 2
    bw = hb * hw
    tq = min(512, seq)
    nq = seq // tq
    v_col0 = col_v * (width // bw)
    lam_specs = [pl.BlockSpec((1, dh), lambda b, h, i: (0, 0)) for _ in range(4)]
    return pl.pallas_call(
        functools.partial(_diff_attn_kernel, lam_init=lam_init, dh=dh),
        grid=(batch, width // bw, nq),
        in_specs=[
            pl.BlockSpec((tq, bw), lambda b, h, i: (b * nq + i, h)),
            pl.BlockSpec((seq, bw), lambda b, h, i: (b, h)),
            pl.BlockSpec((seq, bw), lambda b, h, i: (b, v_col0 + h)),
            *lam_specs,
            pl.BlockSpec((1, hw), lambda b, h, i: (0, 0)),
        ],
        out_specs=pl.BlockSpec((tq, bw), lambda b, h, i: (b * nq + i, h)),
        out_shape=jax.ShapeDtypeStruct((t, width), BF16),
        scratch_shapes=[
            pltpu.VMEM((2 * hb, 1, tq), F32),
            pltpu.VMEM((2 * hb, 1, tq), F32),
            pltpu.VMEM((2 * hb, hw, tq), F32),
            pltpu.VMEM((2, 2 * hb, tq, tq), F32),
        ],
        compiler_params=_params("parallel", "parallel", "arbitrary"),
        name="diff_attn",
    )(qn, kn, zmain, *[p.reshape(1, dh) for p in lam_params], g_dsub.reshape(1, hw))


def _stick_kernel(q_ref, k_ref, v_ref, out_ref, carry_scr, acc_scr, *, dh, sub, far):
    tq = q_ref.shape[0]
    heads = q_ref.shape[1] // dh
    i = pl.program_id(2)
    sign_bit = jnp.uint32(1 << 31)

    srow = lax.broadcasted_iota(jnp.int32, (sub, 2 * sub), 0)
    jcol = lax.broadcasted_iota(jnp.int32, (sub, 2 * sub), 1)
    later = ((jcol & (sub - 1)) > srow).astype(BF16)

    def softplus2(z2):
        neg_abs = pltpu.bitcast(pltpu.bitcast(z2, jnp.uint32) | sign_bit, F32)
        return jnp.maximum(z2, 0.0) + jnp.log(1.0 + jnp.exp2(neg_abs)) * LOG2E

    def suffix_in_block(sp2):
        hi = sp2.astype(BF16)
        lo = (sp2 - hi.astype(F32)).astype(BF16)
        return _dot(later, jnp.concatenate([hi, lo], axis=0))

    def own_keys(q, lanes):
        start = pl.multiple_of(i * tq, tq)
        z_tile = _dot_nt(k_ref[pl.ds(start, tq), lanes], q)
        carry = jnp.zeros((1, tq), F32)
        key_rows = []
        for c in reversed(range(tq // sub)):
            lane0 = c * sub
            z2 = z_tile[lane0:lane0 + sub, lane0:]
            before = (lax.broadcasted_iota(jnp.int32, z2.shape, 0)
                      < lax.broadcasted_iota(jnp.int32, z2.shape, 1))
            sp2 = jnp.where(before, softplus2(z2), 0.0)
            a = jnp.exp2(z2 - sp2 - (suffix_in_block(sp2) + carry[:, lane0:]))
            a = jnp.where(before, a, 0.0).astype(BF16)
            total = jnp.sum(sp2, axis=0, keepdims=True)
            if lane0:
                a = jnp.concatenate([jnp.zeros((sub, lane0), BF16), a], axis=1)
                total = jnp.concatenate([jnp.zeros((1, lane0), F32), total], axis=1)
            key_rows.insert(0, a)
            carry = carry + total
        return _dot_tn(v_ref[pl.ds(start, tq), lanes], jnp.concatenate(key_rows, axis=0)), carry

    def far_keys(q, lanes, jb, carry, valid):
        start = pl.multiple_of(jb * far, far)
        z2 = _dot_nt(k_ref[pl.ds(start, far), lanes], q)
        sp2 = softplus2(z2)
        if valid is not None:
            sp2 = jnp.where(valid, sp2, 0.0)
        parts = []
        for c in reversed(range(far // sub)):
            rows = slice(c * sub, (c + 1) * sub)
            a = jnp.exp2(z2[rows] - sp2[rows] - (suffix_in_block(sp2[rows]) + carry))
            if valid is not None:
                a = jnp.where(valid, a, 0.0)
            parts.insert(0, a.astype(BF16))
            carry = carry + jnp.sum(sp2[rows], axis=0, keepdims=True)
        return _dot_tn(v_ref[pl.ds(start, far), lanes], jnp.concatenate(parts, axis=0)), carry

    jb = i * (tq // far) - 1
    min_carry = []
    for h in range(heads):
        lanes = slice(h * dh, (h + 1) * dh)
        q = q_ref[:, lanes]
        acc, carry = own_keys(q, lanes)
        near, carry = far_keys(q, lanes, jnp.maximum(jb, 0), carry, i > 0)
        acc_scr[h] = acc + near
        carry_scr[h] = carry
        min_carry.append(jnp.min(carry))

    def keep_going(state):
        jb, min_carry = state
        return jnp.logical_and(jb >= 0, min_carry <= STICK_DEAD_LOG2)

    for h in range(heads):
        lanes = slice(h * dh, (h + 1) * dh)

        def body(state, h=h, lanes=lanes):
            jb, _ = state
            more, carry = far_keys(q_ref[:, lanes], lanes, jb, carry_scr[h], None)
            acc_scr[h] += more
            carry_scr[h] = carry
            return jb - 1, jnp.min(carry)

        lax.while_loop(keep_going, body, (jb - 1, min_carry[h]))
        out_ref[:, lanes] = acc_scr[h].T.astype(out_ref.dtype)


def _stick_breaking(zmain, col_q, col_k, col_v, batch, seq, width):
    t = zmain.shape[0]
    dh = width // SB_HEADS
    hb = 2
    tq = min(512, seq)
    nq = seq // tq
    per = width // (hb * dh)
    return pl.pallas_call(
        functools.partial(_stick_kernel, dh=dh, sub=LANE, far=min(256, seq)),
        grid=(batch, SB_HEADS // hb, nq),
        in_specs=[
            pl.BlockSpec((tq, hb * dh), lambda b, h, i: (b * nq + i, col_q * per + h)),
            pl.BlockSpec((seq, hb * dh), lambda b, h, i: (b, col_k * per + h)),
            pl.BlockSpec((seq, hb * dh), lambda b, h, i: (b, col_v * per + h)),
        ],
        out_specs=pl.BlockSpec((tq, hb * dh), lambda b, h, i: (b * nq + i, h)),
        out_shape=jax.ShapeDtypeStruct((t, width), BF16),
        scratch_shapes=[
            pltpu.VMEM((hb, 1, tq), F32),
            pltpu.VMEM((hb, dh, tq), F32),
        ],
        compiler_params=_params("parallel", "parallel", "arbitrary"),
        name="stick_breaking",
    )(zmain, zmain, zmain)


def _merge_kernel(om_ref, od_ref, os_ref, wm_ref, wd_ref, ws_ref, gm_ref, gd_ref, gs_ref, o_ref, w_scr):
    @pl.when(pl.program_id(1) == 0)
    def _():
        for br, w_ref in enumerate((wm_ref, wd_ref, ws_ref)):
            w_scr[br] = w_ref[...].astype(BF16)

    acc = jax.nn.sigmoid(gm_ref[...].astype(F32)) * _dot(om_ref[...], w_scr[0])
    acc = acc + jax.nn.sigmoid(gd_ref[...].astype(F32)) * _dot(od_ref[...], w_scr[1])
    acc = acc + jax.nn.sigmoid(gs_ref[...].astype(F32)) * _dot(os_ref[...], w_scr[2])
    o_ref[...] = acc.astype(o_ref.dtype)


def _merge(out_m, out_d, out_s, w_branch, layer, zmain, gate_col0, d_model):
    t, width = out_m.shape
    tm = min(1024, t)
    tn = 512
    nj = d_model // tn
    act_spec = pl.BlockSpec((tm, width), lambda j, i: (i, 0))
    w_spec = lambda br: pl.BlockSpec((None, None, width, tn), lambda j, i: (layer, br, 0, j))
    gate_blk0 = gate_col0 // tn
    gate_spec = lambda br: pl.BlockSpec((tm, tn), lambda j, i: (i, gate_blk0 + br * nj + j))
    return pl.pallas_call(
        _merge_kernel,
        grid=(nj, t // tm),
        in_specs=[act_spec, act_spec, act_spec, w_spec(0), w_spec(1), w_spec(2),
                  gate_spec(0), gate_spec(1), gate_spec(2)],
        out_specs=pl.BlockSpec((tm, tn), lambda j, i: (i, j)),
        out_shape=jax.ShapeDtypeStruct((t, d_model), BF16),
        scratch_shapes=[pltpu.VMEM((3, width, tn), BF16)],
        compiler_params=_params("parallel", "arbitrary"),
        name="branch_merge",
    )(out_m, out_d, out_s, w_branch, w_branch, w_branch, zmain, zmain, zmain)


def _proj_residual_kernel(a_ref, w_ref, x_ref, gt_ref, o_ref, w_scr):
    @pl.when(pl.program_id(1) == 0)
    def _():
        w_scr[...] = w_ref[...].astype(BF16)

    o_ref[...] = x_ref[...] + gt_ref[...] * _dot(a_ref[...], w_scr[...])


def _proj_residual(a, w, layer, x2, mod, j_gate, seq, tm, tn, name):
    t, k = a.shape
    n = w.shape[2]
    tm = min(tm, seq)
    per_b = seq // tm
    return pl.pallas_call(
        _proj_residual_kernel,
        grid=(n // tn, t // tm),
        in_specs=[
            pl.BlockSpec((tm, k), lambda j, i: (i, 0)),
            pl.BlockSpec((None, k, tn), lambda j, i: (layer, 0, j)),
            pl.BlockSpec((tm, tn), lambda j, i: (i, j)),
            pl.BlockSpec((None, None, 1, tn), lambda j, i: (i // per_b, j_gate, 0, j)),
        ],
        out_specs=pl.BlockSpec((tm, tn), lambda j, i: (i, j)),
        out_shape=jax.ShapeDtypeStruct((t, n), F32),
        scratch_shapes=[pltpu.VMEM((k, tn), BF16)],
        compiler_params=_params("parallel", "arbitrary"),
        name=name,
    )(a, w, x2, mod)


def _ffn_up_kernel(h_ref, wg_ref, wu_ref, wc_ref, o_ref, tail_scr, wg_scr, wu_scr, *, tiles_per_seq):
    tm = h_ref.shape[0]
    i = pl.program_id(1)

    @pl.when(i == 0)
    def _():
        wg_scr[...] = wg_ref[...].astype(BF16)
        wu_scr[...] = wu_ref[...].astype(BF16)

    @pl.when(i % tiles_per_seq == 0)
    def _():
        tail_scr[...] = jnp.zeros_like(tail_scr)

    h = h_ref[...]
    gate = _dot(h, wg_scr[...])
    up = _dot(h, wu_scr[...])
    wc = wc_ref[...]
    xp = jnp.concatenate([tail_scr[...], gate], axis=0)
    conv = gate * wc[FF_CONV - 1:FF_CONV]
    for back in range(1, FF_CONV):
        conv = conv + pltpu.roll(xp, back, 0)[CONV_HALO:] * wc[FF_CONV - 1 - back:FF_CONV - back]
    tail_scr[...] = gate[tm - CONV_HALO:]
    o_ref[...] = (_silu(conv) * up).astype(o_ref.dtype)


def _ffn_up(h2, w_up, w_ffconv, layer, seq):
    t, k = h2.shape
    d_ff = w_ffconv.shape[2]
    tm = min(1024, seq)
    tn = 512
    nj = d_ff // tn
    return pl.pallas_call(
        functools.partial(_ffn_up_kernel, tiles_per_seq=seq // tm),
        grid=(nj, t // tm),
        in_specs=[
            pl.BlockSpec((tm, k), lambda j, i: (i, 0)),
            pl.BlockSpec((None, k, tn), lambda j, i: (layer, 0, j)),
            pl.BlockSpec((None, k, tn), lambda j, i: (layer, 0, nj + j)),
            pl.BlockSpec((None, FF_CONV, tn), lambda j, i: (layer, 0, j)),
        ],
        out_specs=pl.BlockSpec((tm, tn), lambda j, i: (i, j)),
        out_shape=jax.ShapeDtypeStruct((t, d_ff), BF16),
        scratch_shapes=[pltpu.VMEM((CONV_HALO, tn), F32), pltpu.VMEM((k, tn), BF16),
                        pltpu.VMEM((k, tn), BF16)],
        compiler_params=_params("parallel", "arbitrary"),
        name="ffn_up",
    )(h2, w_up, w_up, w_ffconv)


def kernel(x, c, w_ada, b_ada, g_mix, g_ffn, w_in, b_gate_if, w_mconv, g_mout, g_dq, g_dk,
           lam_q1, lam_k1, lam_q2, lam_k2, g_dsub, w_branch, w_out, w_up, w_ffconv, w_down):
    batch, seq, d = x.shape
    depth = w_ada.shape[0]
    width = d // 2
    t = batch * seq
    n_gate = 2 * M_HEADS

    mod_all = _ada_mod(c, w_ada, b_ada)
    x2 = x.reshape(t, d)
    w_in_t = jnp.swapaxes(w_in, 1, 2)

    for l in range(depth):
        mod = mod_all[l]
        gate0 = 4 * width
        sb_scale = (width // SB_HEADS) ** -0.5 * LOG2E
        b_if = jnp.zeros((1, LANE), F32).at[0, :n_gate].set(b_gate_if[l])

        h = _norm_mod(x2, g_mix[l], mod, 1, 0, seq)
        zmain = _in_proj(h, w_in_t, l, width, n_gate, 7, sb_scale)
        zif = _gate_proj(h, w_in_t, l, gate0, b_if, n_gate)

        out_m = _mlstm(zmain, zif, w_mconv[l], g_mout[l], batch, seq, width)
        qn, kn = _qk_norm(zmain, g_dq[l], g_dk[l], width, 4, 5, seq)
        lam_init = 0.8 - 0.6 * math.exp(-0.3 * l)
        out_d = _diff_attn(qn, kn, zmain, 6, (lam_q1[l], lam_k1[l], lam_q2[l], lam_k2[l]),
                           g_dsub[l], lam_init, batch, seq, width)
        out_s = _stick_breaking(zmain, 7, 8, 9, batch, seq, width)

        merged = _merge(out_m, out_d, out_s, w_branch, l, zmain, 10 * width, d)
        x2 = _proj_residual(merged, w_out, l, x2, mod, 2, seq, 1024, 1024, "out_proj")

        h2 = _norm_mod(x2, g_ffn[l], mod, 4, 3, seq)
        act = _ffn_up(h2, w_up, w_ffconv, l, seq)
        x2 = _proj_residual(act, w_down, l, x2, mod, 5, seq, 512, 512, "ffn_down")

    return x2.reshape(batch, seq, d)
```

```python
import functools
import math

import jax
import jax.numpy as jnp
from jax import lax
from jax.experimental import pallas as pl
from jax.experimental.pallas import tpu as pltpu

EPS = 1e-6
M_HEADS = 4
M_CONV = 4
M_CHUNK = 128
DA_HEADS = 8
SB_HEADS = 8
N_MOD = 6
FF_CONV = 3
LANE = 128
CONV_HALO = 8
BF16_SUBLANES = 16
VMEM_LIMIT_BYTES = 56 * 1024 * 1024
HIGHEST = lax.Precision.HIGHEST
NEG_INF = float("-inf")
LOG2E = math.log2(math.e)
STICK_DEAD_LOG2 = 152.0

BF16 = jnp.bfloat16
F32 = jnp.float32


def _params(*sem):
    return pltpu.CompilerParams(dimension_semantics=sem, vmem_limit_bytes=VMEM_LIMIT_BYTES)


def _dot(a, b):
    return jnp.dot(a, b, preferred_element_type=F32)


def _dot_nt(a, b):
    return lax.dot_general(a, b, (((1,), (1,)), ((), ())), preferred_element_type=F32)


def _dot_tn(a, b):
    return lax.dot_general(a, b, (((0,), (0,)), ((), ())), preferred_element_type=F32)


def _silu(x):
    return x * jax.nn.sigmoid(x)


def _softplus(x):
    return jnp.maximum(x, 0.0) + jnp.log1p(jnp.exp(-jnp.abs(x)))


def _log_sigmoid(x):
    return -_softplus(-x)


def _ada_kernel(c_ref, w_ref, b_ref, o_ref):
    c = c_ref[...]
    o_ref[...] = jnp.dot(_silu(c), w_ref[...], preferred_element_type=F32,
                         precision=HIGHEST) + b_ref[...]


def _ada_mod(c, w_ada, b_ada):
    n_layers, d, n = w_ada.shape
    b = c.shape[0]
    rows = 8
    c_pad = jnp.zeros((rows, d), F32).at[:b].set(c)
    tn = 1024
    out = pl.pallas_call(
        _ada_kernel,
        grid=(n_layers, n // tn),
        in_specs=[
            pl.BlockSpec((rows, d), lambda l, j: (0, 0)),
            pl.BlockSpec((None, d, tn), lambda l, j: (l, 0, j)),
            pl.BlockSpec((None, 1, tn), lambda l, j: (l, 0, j)),
        ],
        out_specs=pl.BlockSpec((None, rows, tn), lambda l, j: (l, 0, j)),
        out_shape=jax.ShapeDtypeStruct((n_layers, rows, n), F32),
        compiler_params=_params("parallel", "parallel"),
        name="ada_mod",
    )(c_pad, w_ada, b_ada.reshape(n_layers, 1, n))
    return out[:, :b].reshape(n_layers, b, N_MOD, 1, d)


def _norm_mod_kernel(x_ref, g_ref, sc_ref, sh_ref, o_ref):
    x = x_ref[...]
    y = x * lax.rsqrt(jnp.mean(x * x, axis=-1, keepdims=True) + EPS) * g_ref[...]
    o_ref[...] = (y * (1.0 + sc_ref[...]) + sh_ref[...]).astype(o_ref.dtype)


def _norm_mod(x2, g, mod, j_scale, j_shift, seq):
    t, d = x2.shape
    ts = min(512, seq)
    per_b = seq // ts
    return pl.pallas_call(
        _norm_mod_kernel,
        grid=(t // ts,),
        in_specs=[
            pl.BlockSpec((ts, d), lambda i: (i, 0)),
            pl.BlockSpec((1, d), lambda i: (0, 0)),
            pl.BlockSpec((None, None, 1, d), lambda i: (i // per_b, j_scale, 0, 0)),
            pl.BlockSpec((None, None, 1, d), lambda i: (i // per_b, j_shift, 0, 0)),
        ],
        out_specs=pl.BlockSpec((ts, d), lambda i: (i, 0)),
        out_shape=jax.ShapeDtypeStruct((t, d), BF16),
        compiler_params=_params("parallel"),
        name="norm_mod",
    )(x2, g.reshape(1, d), mod, mod)


def _in_proj_kernel(h_ref, wt_ref, o_ref, w_scr, *, scaled_block, scale):
    m = pl.program_id(0)

    @pl.when(pl.program_id(1) == 0)
    def _():
        w = wt_ref[0] * jnp.where(m == scaled_block, scale, 1.0)
        w_scr[...] = w.T.astype(BF16)

    o_ref[...] = _dot(h_ref[...], w_scr[...]).astype(o_ref.dtype)


def _in_proj(h, w_in_t, layer, width, n_gate, scaled_block, scale):
    t, k = h.shape
    n_out = w_in_t.shape[1] - n_gate
    tm = min(1024, t)
    tn = width
    aligned_blocks = 4
    row0 = lambda m: pl.multiple_of(m * tn + jnp.where(m >= aligned_blocks, n_gate, 0), n_gate)
    return pl.pallas_call(
        functools.partial(_in_proj_kernel, scaled_block=scaled_block, scale=scale),
        grid=(n_out // tn, t // tm),
        in_specs=[
            pl.BlockSpec((tm, k), lambda m, i: (i, 0)),
            pl.BlockSpec((pl.Element(1), pl.Element(tn), pl.Element(k)),
                         lambda m, i: (layer, row0(m), 0)),
        ],
        out_specs=pl.BlockSpec((tm, tn), lambda m, i: (i, m)),
        out_shape=jax.ShapeDtypeStruct((t, n_out), BF16),
        scratch_shapes=[pltpu.VMEM((k, tn), BF16)],
        compiler_params=_params("parallel", "arbitrary"),
        name="in_proj",
    )(h, w_in_t)


def _gate_proj_kernel(a_ref, wt_ref, b_ref, o_ref):
    wt = wt_ref[...]
    pad = jnp.zeros((o_ref.shape[1] - wt.shape[0], wt.shape[1]), wt.dtype)
    w = jnp.concatenate([wt, pad], axis=0).astype(BF16)
    o_ref[...] = _dot_nt(a_ref[...], w) + b_ref[...]


def _gate_proj(a, w_in_t, layer, gate0, bias, n_gate):
    t, k = a.shape
    tm = min(1024, t)
    return pl.pallas_call(
        _gate_proj_kernel,
        grid=(t // tm,),
        in_specs=[
            pl.BlockSpec((tm, k), lambda i: (i, 0)),
            pl.BlockSpec((None, n_gate, k), lambda i: (layer, gate0 // n_gate, 0)),
            pl.BlockSpec((1, LANE), lambda i: (0, 0)),
        ],
        out_specs=pl.BlockSpec((tm, LANE), lambda i: (i, 0)),
        out_shape=jax.ShapeDtypeStruct((t, LANE), F32),
        compiler_params=_params("parallel"),
        name="gate_proj",
    )(a, w_in_t, bias)


def _mlstm_kernel(q_ref, k_ref, v_ref, o_ref, zif_ref, wconv_ref, g_ref, out_ref,
                  c_scr, n_scr, m_scr, qtail_scr, ktail_scr, shift_scr, *, dh):
    chunk = q_ref.shape[0]
    width = q_ref.shape[1]
    heads = width // dh
    halo = qtail_scr.shape[0]

    @pl.when(pl.program_id(1) == 0)
    def _():
        c_scr[...] = jnp.zeros_like(c_scr)
        n_scr[...] = jnp.zeros_like(n_scr)
        m_scr[...] = jnp.zeros_like(m_scr)
        qtail_scr[...] = jnp.zeros_like(qtail_scr)
        ktail_scr[...] = jnp.zeros_like(ktail_scr)
        srow = lax.broadcasted_iota(jnp.int32, shift_scr.shape, 0)
        scol = lax.broadcasted_iota(jnp.int32, shift_scr.shape, 1)
        delay = srow >> (chunk.bit_length() - 1)
        shift_scr[...] = (scol - halo == (srow & (chunk - 1)) - delay).astype(BF16)

    wconv = wconv_ref[...]
    shift = shift_scr[...]

    def conv_silu(x_ref, tail_scr, w):
        raw = x_ref[...]
        delayed = _dot(shift, jnp.concatenate([tail_scr[...], raw], axis=0))
        tail_scr[...] = raw[chunk - halo:]
        acc = delayed[:chunk] * w[M_CONV - 1:M_CONV]
        for back in range(1, M_CONV):
            acc = acc + delayed[back * chunk:(back + 1) * chunk] * w[M_CONV - 1 - back:M_CONV - back]
        return _silu(acc)

    q_all = conv_silu(q_ref, qtail_scr, wconv[:, :width]) * (dh ** -0.5)
    k_all = conv_silu(k_ref, ktail_scr, wconv[:, width:])

    row = lax.broadcasted_iota(jnp.int32, (chunk, chunk), 0)
    col = lax.broadcasted_iota(jnp.int32, (chunk, chunk), 1)
    causal = col <= row

    zif = zif_ref[...]
    lsf = _log_sigmoid(zif)
    p1 = lsf.astype(BF16)
    r1 = lsf - p1.astype(F32)
    p2 = r1.astype(BF16)
    p3 = (r1 - p2.astype(F32)).astype(BF16)
    tri = causal.astype(BF16)
    b_all = _dot(jnp.concatenate([tri, tri, tri], axis=1), jnp.concatenate([p1, p2, p3], axis=0))
    b_all_t = b_all.T
    zif_t = zif.T

    for h in range(heads):
        sl = slice(h * dh, (h + 1) * dh)
        qh = q_all[:, sl]
        kh = k_all[:, sl]
        qb = qh.astype(BF16)
        kb = kh.astype(BF16)
        vb = v_ref[:, sl]

        i_col = zif[:, h:h + 1]
        i_row = zif_t[h:h + 1, :]
        b_col = b_all[:, heads + h:heads + h + 1]
        b_row = b_all_t[heads + h:heads + h + 1, :]
        b_last = b_row[:, chunk - 1:chunk]

        m_prev = m_scr[h][:, :1]
        dmat = jnp.where(causal, b_col - b_row + i_row, NEG_INF)
        m_inter = b_col + m_prev
        m_t = jnp.maximum(jnp.max(dmat, axis=-1, keepdims=True), m_inter)
        w = _dot_nt(qb, kb) * jnp.exp(dmat - m_t)
        decay = jnp.exp(m_inter - m_t)
        c_prev = c_scr[h]
        n_prev = n_scr[h]
        num = _dot(w.astype(BF16), vb) + decay * _dot(qb, c_prev.astype(BF16))
        den = jnp.sum(w, axis=-1, keepdims=True) + decay * jnp.sum(qh * n_prev, axis=-1, keepdims=True)
        hh = num / jnp.maximum(jnp.abs(den), jnp.exp(-m_t))

        hn = hh * lax.rsqrt(jnp.mean(hh * hh, axis=-1, keepdims=True) + EPS) * g_ref[:, sl]
        out_ref[:, sl] = (hn * jax.nn.sigmoid(o_ref[:, sl].astype(F32))).astype(out_ref.dtype)

        g_col = b_last - b_col + i_col
        m_new = jnp.maximum(b_last + m_prev, jnp.max(g_col, axis=0, keepdims=True))
        wk = jnp.exp(g_col - m_new)
        carry_decay = jnp.exp(b_last + m_prev - m_new)
        kw = kh * wk
        c_scr[h] = carry_decay * c_prev + _dot_tn(kw.astype(BF16), vb)
        n_scr[h] = carry_decay * n_prev + jnp.sum(kw, axis=0, keepdims=True)
        m_scr[h] = jnp.broadcast_to(m_new, m_scr.shape[1:])


def _mlstm(zmain, zif, w_mconv, g_mout, batch, seq, width):
    t = zmain.shape[0]
    dh = width // M_HEADS
    nc = seq // M_CHUNK
    row_map = lambda cb: (lambda b, c: (b * nc + c, cb))
    return pl.pallas_call(
        functools.partial(_mlstm_kernel, dh=dh),
        grid=(batch, nc),
        in_specs=[
            pl.BlockSpec((M_CHUNK, width), row_map(0)),
            pl.BlockSpec((M_CHUNK, width), row_map(1)),
            pl.BlockSpec((M_CHUNK, width), row_map(2)),
            pl.BlockSpec((M_CHUNK, width), row_map(3)),
            pl.BlockSpec((M_CHUNK, LANE), row_map(0)),
            pl.BlockSpec((M_CONV, 2 * width), lambda b, c: (0, 0)),
            pl.BlockSpec((1, width), lambda b, c: (0, 0)),
        ],
        out_specs=pl.BlockSpec((M_CHUNK, width), row_map(0)),
        out_shape=jax.ShapeDtypeStruct((t, width), BF16),
        scratch_shapes=[
            pltpu.VMEM((M_HEADS, dh, dh), F32),
            pltpu.VMEM((M_HEADS, 1, dh), F32),
            pltpu.VMEM((M_HEADS, 1, LANE), F32),
            pltpu.VMEM((BF16_SUBLANES, width), BF16),
            pltpu.VMEM((BF16_SUBLANES, width), BF16),
            pltpu.VMEM((M_CONV * M_CHUNK, BF16_SUBLANES + M_CHUNK), BF16),
        ],
        compiler_params=_params("parallel", "arbitrary"),
        name="mlstm",
    )(zmain, zmain, zmain, zmain, zif, w_mconv, g_mout.reshape(1, width))


def _qk_norm_kernel(q_ref, k_ref, gq_ref, gk_ref, qo_ref, ko_ref, *, dh):
    width = q_ref.shape[1]
    row = lax.broadcasted_iota(jnp.int32, (LANE, LANE), 0)
    col = lax.broadcasted_iota(jnp.int32, (LANE, LANE), 1)
    same_group = ((row // dh) == (col // dh)).astype(BF16)

    def group_rms(x_ref, g_ref, o_ref, scale):
        for tile in range(width // LANE):
            sl = slice(tile * LANE, (tile + 1) * LANE)
            x = x_ref[:, sl].astype(F32)
            sq = x * x
            hi = sq.astype(BF16)
            lo = (sq - hi.astype(F32)).astype(BF16)
            ssum = _dot(hi, same_group) + _dot(lo, same_group)
            y = x * lax.rsqrt(ssum * (1.0 / dh) + EPS) * g_ref[:, sl]
            o_ref[:, sl] = (y * scale).astype(o_ref.dtype)

    group_rms(q_ref, gq_ref, qo_ref, dh ** -0.5 * LOG2E)
    group_rms(k_ref, gk_ref, ko_ref, 1.0)


def _qk_norm(zmain, g_dq, g_dk, width, col_q, col_k, seq):
    t = zmain.shape[0]
    dh = g_dq.shape[0]
    ts = min(512, seq)
    reps = width // dh
    return pl.pallas_call(
        functools.partial(_qk_norm_kernel, dh=dh),
        grid=(t // ts,),
        in_specs=[
            pl.BlockSpec((ts, width), lambda i: (i, col_q)),
            pl.BlockSpec((ts, width), lambda i: (i, col_k)),
            pl.BlockSpec((1, width), lambda i: (0, 0)),
            pl.BlockSpec((1, width), lambda i: (0, 0)),
        ],
        out_specs=[pl.BlockSpec((ts, width), lambda i: (i, 0)),
                   pl.BlockSpec((ts, width), lambda i: (i, 0))],
        out_shape=[jax.ShapeDtypeStruct((t, width), BF16), jax.ShapeDtypeStruct((t, width), BF16)],
        compiler_params=_params("parallel"),
        name="qk_norm",
    )(zmain, zmain, jnp.tile(g_dq, reps).reshape(1, width), jnp.tile(g_dk, reps).reshape(1, width))


def _diff_attn_kernel(q_ref, k_ref, v_ref, lq1_ref, lk1_ref, lq2_ref, lk2_ref, g_ref, out_ref,
                      m_scr, l_scr, acc_scr, s_scr, *, lam_init, dh):
    tq = q_ref.shape[0]
    hw = 2 * dh
    heads = q_ref.shape[1] // hw
    i = pl.program_id(2)
    head_lanes = [slice(h * hw, (h + 1) * hw) for h in range(heads)]

    q_streams = []
    for lanes in head_lanes:
        q = q_ref[:, lanes]
        lane = lax.broadcasted_iota(jnp.int32, q.shape, 1)
        zero = jnp.zeros_like(q)
        q_streams += [jnp.where(lane < dh, q, zero), jnp.where(lane >= dh, q, zero)]

    m_scr[...] = jnp.full_like(m_scr, NEG_INF)
    l_scr[...] = jnp.zeros_like(l_scr)
    acc_scr[...] = jnp.zeros_like(acc_scr)

    def key_block(ref, j, lanes):
        return ref[pl.ds(pl.multiple_of(j * tq, tq), tq), lanes]

    def put_scores(j, slot):
        for h, lanes in enumerate(head_lanes):
            k = key_block(k_ref, j, lanes)
            for c in range(2):
                s_scr[slot, 2 * h + c] = _dot_nt(k, q_streams[2 * h + c])

    def consume(j, slot, masked):
        probs, alphas = [], []
        for u in range(2 * heads):
            s = s_scr[slot, u]
            if masked:
                key = lax.broadcasted_iota(jnp.int32, s.shape, 0)
                qry = lax.broadcasted_iota(jnp.int32, s.shape, 1)
                s = jnp.where(key <= qry, s, NEG_INF)
            m_prev = m_scr[u]
            m_new = jnp.maximum(m_prev, jnp.max(s, axis=0, keepdims=True))
            alpha = jnp.exp2(m_prev - m_new)
            p = jnp.exp2(s - m_new)
            l_scr[u] = alpha * l_scr[u] + jnp.sum(p, axis=0, keepdims=True)
            m_scr[u] = m_new
            probs.append(p.astype(BF16))
            alphas.append(alpha)
        for h, lanes in enumerate(head_lanes):
            v = key_block(v_ref, j, lanes)
            for u in (2 * h, 2 * h + 1):
                acc_scr[u] = alphas[u] * acc_scr[u] + _dot_tn(v, probs[u])

    put_scores(0, 0)

    def pair(p, carry):
        j = 2 * p
        put_scores(j + 1, 1)
        consume(j, 0, False)
        put_scores(j + 2, 0)
        consume(j + 1, 1, False)
        return carry

    lax.fori_loop(0, i // 2, pair, 0)

    @pl.when(i % 2 == 0)
    def _():
        consume(i, 0, True)

    @pl.when(i % 2 == 1)
    def _():
        put_scores(i, 1)
        consume(i - 1, 0, False)
        consume(i, 1, True)

    lam = (jnp.exp(jnp.sum(lq1_ref[...] * lk1_ref[...], axis=-1, keepdims=True))
           - jnp.exp(jnp.sum(lq2_ref[...] * lk2_ref[...], axis=-1, keepdims=True)) + lam_init)
    for h, lanes in enumerate(head_lanes):
        u = 2 * h
        o = (acc_scr[u] / l_scr[u] - lam * (acc_scr[u + 1] / l_scr[u + 1])).T
        on = o * lax.rsqrt(jnp.mean(o * o, axis=-1, keepdims=True) + EPS) * g_ref[...]
        out_ref[:, lanes] = (on * (1.0 - lam_init)).astype(out_ref.dtype)


def _diff_attn(qn, kn, zmain, col_v, lam_params, g_dsub, lam_init, batch, seq, width):
    t = qn.shape[0]
    dh = lam_params[0].shape[0]
    hw = 2 * dh
    hb = 4
    bw = hb * hw
    tq = min(512, seq)
    nq = seq // tq
    v_col0 = col_v * (width // bw)
    lam_specs = [pl.BlockSpec((1, dh), lambda b, h, i: (0, 0)) for _ in range(4)]
    return pl.pallas_call(
        functools.partial(_diff_attn_kernel, lam_init=lam_init, dh=dh),
        grid=(batch, width // bw, nq),
        in_specs=[
            pl.BlockSpec((tq, bw), lambda b, h, i: (b * nq + i, h)),
            pl.BlockSpec((seq, bw), lambda b, h, i: (b, h)),
            pl.BlockSpec((seq, bw), lambda b, h, i: (b, v_col0 + h)),
            *lam_specs,
            pl.BlockSpec((1, hw), lambda b, h, i: (0, 0)),
        ],
        out_specs=pl.BlockSpec((tq, bw), lambda b, h, i: (b * nq + i, h)),
        out_shape=jax.ShapeDtypeStruct((t, width), BF16),
        scratch_shapes=[
            pltpu.VMEM((2 * hb, 1, tq), F32),
            pltpu.VMEM((2 * hb, 1, tq), F32),
            pltpu.VMEM((2 * hb, hw, tq), F32),
            pltpu.VMEM((2, 2 * hb, tq, tq), F32),
        ],
        compiler_params=_params("parallel", "parallel", "arbitrary"),
        name="diff_attn",
    )(qn, kn, zmain, *[p.reshape(1, dh) for p in lam_params], g_dsub.reshape(1, hw))


def _stick_kernel(q_ref, k_ref, v_ref, out_ref, carry_scr, acc_scr, *, dh, sub, far):
    tq = q_ref.shape[0]
    heads = q_ref.shape[1] // dh
    i = pl.program_id(2)
    sign_bit = jnp.uint32(1 << 31)

    srow = lax.broadcasted_iota(jnp.int32, (sub, 2 * sub), 0)
    jcol = lax.broadcasted_iota(jnp.int32, (sub, 2 * sub), 1)
    later = ((jcol & (sub - 1)) > srow).astype(BF16)

    def softplus2(z2):
        neg_abs = pltpu.bitcast(pltpu.bitcast(z2, jnp.uint32) | sign_bit, F32)
        return jnp.maximum(z2, 0.0) + jnp.log(1.0 + jnp.exp2(neg_abs)) * LOG2E

    def suffix_in_block(sp2):
        hi = sp2.astype(BF16)
        lo = (sp2 - hi.astype(F32)).astype(BF16)
        return _dot(later, jnp.concatenate([hi, lo], axis=0))

    def own_keys(q, lanes):
        start = pl.multiple_of(i * tq, tq)
        z_tile = _dot_nt(k_ref[pl.ds(start, tq), lanes], q)
        carry = jnp.zeros((1, tq), F32)
        key_rows = []
        for c in reversed(range(tq // sub)):
            lane0 = c * sub
            z2 = z_tile[lane0:lane0 + sub, lane0:]
            before = (lax.broadcasted_iota(jnp.int32, z2.shape, 0)
                      < lax.broadcasted_iota(jnp.int32, z2.shape, 1))
            sp2 = jnp.where(before, softplus2(z2), 0.0)
            a = jnp.exp2(z2 - sp2 - (suffix_in_block(sp2) + carry[:, lane0:]))
            a = jnp.where(before, a, 0.0).astype(BF16)
            total = jnp.sum(sp2, axis=0, keepdims=True)
            if lane0:
                a = jnp.concatenate([jnp.zeros((sub, lane0), BF16), a], axis=1)
                total = jnp.concatenate([jnp.zeros((1, lane0), F32), total], axis=1)
            key_rows.insert(0, a)
            carry = carry + total
        return _dot_tn(v_ref[pl.ds(start, tq), lanes], jnp.concatenate(key_rows, axis=0)), carry

    def far_keys(q, lanes, jb, carry, valid):
        start = pl.multiple_of(jb * far, far)
        z2 = _dot_nt(k_ref[pl.ds(start, far), lanes], q)
        sp2 = softplus2(z2)
        if valid is not None:
            sp2 = jnp.where(valid, sp2, 0.0)
        parts = []
        for c in reversed(range(far // sub)):
            rows = slice(c * sub, (c + 1) * sub)
            a = jnp.exp2(z2[rows] - sp2[rows] - (suffix_in_block(sp2[rows]) + carry))
            if valid is not None:
                a = jnp.where(valid, a, 0.0)
            parts.insert(0, a.astype(BF16))
            carry = carry + jnp.sum(sp2[rows], axis=0, keepdims=True)
        return _dot_tn(v_ref[pl.ds(start, far), lanes], jnp.concatenate(parts, axis=0)), carry

    jb = i * (tq // far) - 1
    min_carry = []
    for h in range(heads):
        lanes = slice(h * dh, (h + 1) * dh)
        q = q_ref[:, lanes]
        acc, carry = own_keys(q, lanes)
        near, carry = far_keys(q, lanes, jnp.maximum(jb, 0), carry, i > 0)
        acc_scr[h] = acc + near
        carry_scr[h] = carry
        min_carry.append(jnp.min(carry))

    def keep_going(state):
        jb, min_carry = state
        return jnp.logical_and(jb >= 0, min_carry <= STICK_DEAD_LOG2)

    for h in range(heads):
        lanes = slice(h * dh, (h + 1) * dh)

        def body(state, h=h, lanes=lanes):
            jb, _ = state
            more, carry = far_keys(q_ref[:, lanes], lanes, jb, carry_scr[h], None)
            acc_scr[h] += more
            carry_scr[h] = carry
            return jb - 1, jnp.min(carry)

        lax.while_loop(keep_going, body, (jb - 1, min_carry[h]))
        out_ref[:, lanes] = acc_scr[h].T.astype(out_ref.dtype)


def _stick_breaking(zmain, col_q, col_k, col_v, batch, seq, width):
    t = zmain.shape[0]
    dh = width // SB_HEADS
    hb = 4
    tq = min(512, seq)
    nq = seq // tq
    per = width // (hb * dh)
    return pl.pallas_call(
        functools.partial(_stick_kernel, dh=dh, sub=LANE, far=min(256, seq)),
        grid=(batch, SB_HEADS // hb, nq),
        in_specs=[
            pl.BlockSpec((tq, hb * dh), lambda b, h, i: (b * nq + i, col_q * per + h)),
            pl.BlockSpec((seq, hb * dh), lambda b, h, i: (b, col_k * per + h)),
            pl.BlockSpec((seq, hb * dh), lambda b, h, i: (b, col_v * per + h)),
        ],
        out_specs=pl.BlockSpec((tq, hb * dh), lambda b, h, i: (b * nq + i, h)),
        out_shape=jax.ShapeDtypeStruct((t, width), BF16),
        scratch_shapes=[
            pltpu.VMEM((hb, 1, tq), F32),
            pltpu.VMEM((hb, dh, tq), F32),
        ],
        compiler_params=_params("parallel", "parallel", "arbitrary"),
        name="stick_breaking",
    )(zmain, zmain, zmain)


def _merge_kernel(om_ref, od_ref, os_ref, wm_ref, wd_ref, ws_ref, gm_ref, gd_ref, gs_ref, o_ref, w_scr):
    @pl.when(pl.program_id(1) == 0)
    def _():
        for br, w_ref in enumerate((wm_ref, wd_ref, ws_ref)):
            w_scr[br] = w_ref[...].astype(BF16)

    acc = jax.nn.sigmoid(gm_ref[...].astype(F32)) * _dot(om_ref[...], w_scr[0])
    acc = acc + jax.nn.sigmoid(gd_ref[...].astype(F32)) * _dot(od_ref[...], w_scr[1])
    acc = acc + jax.nn.sigmoid(gs_ref[...].astype(F32)) * _dot(os_ref[...], w_scr[2])
    o_ref[...] = acc.astype(o_ref.dtype)


def _merge(out_m, out_d, out_s, w_branch, layer, zmain, gate_col0, d_model):
    t, width = out_m.shape
    tm = min(1024, t)
    tn = 512
    nj = d_model // tn
    act_spec = pl.BlockSpec((tm, width), lambda j, i: (i, 0))
    w_spec = lambda br: pl.BlockSpec((None, None, width, tn), lambda j, i: (layer, br, 0, j))
    gate_blk0 = gate_col0 // tn
    gate_spec = lambda br: pl.BlockSpec((tm, tn), lambda j, i: (i, gate_blk0 + br * nj + j))
    return pl.pallas_call(
        _merge_kernel,
        grid=(nj, t // tm),
        in_specs=[act_spec, act_spec, act_spec, w_spec(0), w_spec(1), w_spec(2),
                  gate_spec(0), gate_spec(1), gate_spec(2)],
        out_specs=pl.BlockSpec((tm, tn), lambda j, i: (i, j)),
        out_shape=jax.ShapeDtypeStruct((t, d_model), BF16),
        scratch_shapes=[pltpu.VMEM((3, width, tn), BF16)],
        compiler_params=_params("parallel", "arbitrary"),
        name="branch_merge",
    )(out_m, out_d, out_s, w_branch, w_branch, w_branch, zmain, zmain, zmain)


def _proj_residual_kernel(a_ref, w_ref, x_ref, gt_ref, o_ref, w_scr):
    @pl.when(pl.program_id(1) == 0)
    def _():
        w_scr[...] = w_ref[...].astype(BF16)

    o_ref[...] = x_ref[...] + gt_ref[...] * _dot(a_ref[...], w_scr[...])


def _proj_residual(a, w, layer, x2, mod, j_gate, seq, tm, tn, name):
    t, k = a.shape
    n = w.shape[2]
    tm = min(tm, seq)
    per_b = seq // tm
    return pl.pallas_call(
        _proj_residual_kernel,
        grid=(n // tn, t // tm),
        in_specs=[
            pl.BlockSpec((tm, k), lambda j, i: (i, 0)),
            pl.BlockSpec((None, k, tn), lambda j, i: (layer, 0, j)),
            pl.BlockSpec((tm, tn), lambda j, i: (i, j)),
            pl.BlockSpec((None, None, 1, tn), lambda j, i: (i // per_b, j_gate, 0, j)),
        ],
        out_specs=pl.BlockSpec((tm, tn), lambda j, i: (i, j)),
        out_shape=jax.ShapeDtypeStruct((t, n), F32),
        scratch_shapes=[pltpu.VMEM((k, tn), BF16)],
        compiler_params=_params("parallel", "arbitrary"),
        name=name,
    )(a, w, x2, mod)


def _ffn_up_kernel(h_ref, wg_ref, wu_ref, wc_ref, o_ref, tail_scr, wg_scr, wu_scr, *, tiles_per_seq):
    tm = h_ref.shape[0]
    i = pl.program_id(1)

    @pl.when(i == 0)
    def _():
        wg_scr[...] = wg_ref[...].astype(BF16)
        wu_scr[...] = wu_ref[...].astype(BF16)

    @pl.when(i % tiles_per_seq == 0)
    def _():
        tail_scr[...] = jnp.zeros_like(tail_scr)

    h = h_ref[...]
    gate = _dot(h, wg_scr[...])
    up = _dot(h, wu_scr[...])
    wc = wc_ref[...]
    xp = jnp.concatenate([tail_scr[...], gate], axis=0)
    conv = gate * wc[FF_CONV - 1:FF_CONV]
    for back in range(1, FF_CONV):
        conv = conv + pltpu.roll(xp, back, 0)[CONV_HALO:] * wc[FF_CONV - 1 - back:FF_CONV - back]
    tail_scr[...] = gate[tm - CONV_HALO:]
    o_ref[...] = (_silu(conv) * up).astype(o_ref.dtype)


def _ffn_up(h2, w_up, w_ffconv, layer, seq):
    t, k = h2.shape
    d_ff = w_ffconv.shape[2]
    tm = min(1024, seq)
    tn = 512
    nj = d_ff // tn
    return pl.pallas_call(
        functools.partial(_ffn_up_kernel, tiles_per_seq=seq // tm),
        grid=(nj, t // tm),
        in_specs=[
            pl.BlockSpec((tm, k), lambda j, i: (i, 0)),
            pl.BlockSpec((None, k, tn), lambda j, i: (layer, 0, j)),
            pl.BlockSpec((None, k, tn), lambda j, i: (layer, 0, nj + j)),
            pl.BlockSpec((None, FF_CONV, tn), lambda j, i: (layer, 0, j)),
        ],
        out_specs=pl.BlockSpec((tm, tn), lambda j, i: (i, j)),
        out_shape=jax.ShapeDtypeStruct((t, d_ff), BF16),
        scratch_shapes=[pltpu.VMEM((CONV_HALO, tn), F32), pltpu.VMEM((k, tn), BF16),
                        pltpu.VMEM((k, tn), BF16)],
        compiler_params=_params("parallel", "arbitrary"),
        name="ffn_up",
    )(h2, w_up, w_up, w_ffconv)


def kernel(x, c, w_ada, b_ada, g_mix, g_ffn, w_in, b_gate_if, w_mconv, g_mout, g_dq, g_dk,
           lam_q1, lam_k1, lam_q2, lam_k2, g_dsub, w_branch, w_out, w_up, w_ffconv, w_down):
    batch, seq, d = x.shape
    depth = w_ada.shape[0]
    width = d // 2
    t = batch * seq
    n_gate = 2 * M_HEADS

    mod_all = _ada_mod(c, w_ada, b_ada)
    x2 = x.reshape(t, d)
    w_in_t = jnp.swapaxes(w_in, 1, 2)

    for l in range(depth):
        mod = mod_all[l]
        gate0 = 4 * width
        sb_scale = (width // SB_HEADS) ** -0.5 * LOG2E
        b_if = jnp.zeros((1, LANE), F32).at[0, :n_gate].set(b_gate_if[l])

        h = _norm_mod(x2, g_mix[l], mod, 1, 0, seq)
        zmain = _in_proj(h, w_in_t, l, width, n_gate, 7, sb_scale)
        zif = _gate_proj(h, w_in_t, l, gate0, b_if, n_gate)

        out_m = _mlstm(zmain, zif, w_mconv[l], g_mout[l], batch, seq, width)
        qn, kn = _qk_norm(zmain, g_dq[l], g_dk[l], width, 4, 5, seq)
        lam_init = 0.8 - 0.6 * math.exp(-0.3 * l)
        out_d = _diff_attn(qn, kn, zmain, 6, (lam_q1[l], lam_k1[l], lam_q2[l], lam_k2[l]),
                           g_dsub[l], lam_init, batch, seq, width)
        out_s = _stick_breaking(zmain, 7, 8, 9, batch, seq, width)

        merged = _merge(out_m, out_d, out_s, w_branch, l, zmain, 10 * width, d)
        x2 = _proj_residual(merged, w_out, l, x2, mod, 2, seq, 1024, 1024, "out_proj")

        h2 = _norm_mod(x2, g_ffn[l], mod, 4, 3, seq)
        act = _ffn_up(h2, w_up, w_ffconv, l, seq)
        x2 = _proj_residual(act, w_down, l, x2, mod, 5, seq, 512, 512, "ffn_down")

    return x2.reshape(batch, seq, d)
```

```python
import functools
import math

import jax
import jax.numpy as jnp
from jax import lax
from jax.experimental import pallas as pl
from jax.experimental.pallas import tpu as pltpu

EPS = 1e-6
M_HEADS = 4
M_CONV = 4
M_CHUNK = 128
DA_HEADS = 8
SB_HEADS = 8
N_MOD = 6
FF_CONV = 3
LANE = 128
F32_SUBLANES = 8
BF16_SUBLANES = 16
CONV_HALO = F32_SUBLANES

ADA_COLS = 1024
NORM_ROWS = 512
PROJ_ROWS = 1024
PROJ_COLS = 1024
HALF_COLS = 512
FFN_DOWN_ROWS = 512
ATTN_ROWS = 512
ATTN_HEADS_PER_STEP = 4
STICK_FAR_KEYS = 256
MLSTM_SEQS_PER_STEP = 2
VMEM_LIMIT_BYTES = 56 * 1024 * 1024
HIGHEST = lax.Precision.HIGHEST
NEG_INF = float("-inf")
LOG2E = math.log2(math.e)
STICK_DEAD_LOG2 = 152.0

BF16 = jnp.bfloat16
F32 = jnp.float32


def _params(*sem):
    return pltpu.CompilerParams(dimension_semantics=sem, vmem_limit_bytes=VMEM_LIMIT_BYTES)


def _dot(a, b):
    return jnp.dot(a, b, preferred_element_type=F32)


def _dot_nt(a, b):
    return lax.dot_general(a, b, (((1,), (1,)), ((), ())), preferred_element_type=F32)


def _dot_tn(a, b):
    return lax.dot_general(a, b, (((0,), (0,)), ((), ())), preferred_element_type=F32)


def _silu(x):
    return x * jax.nn.sigmoid(x)


def _softplus(x):
    return jnp.maximum(x, 0.0) + jnp.log1p(jnp.exp(-jnp.abs(x)))


def _log_sigmoid(x):
    return -_softplus(-x)


def _ada_kernel(c_ref, w_ref, b_ref, o_ref):
    c = c_ref[...]
    o_ref[...] = jnp.dot(_silu(c), w_ref[...], preferred_element_type=F32,
                         precision=HIGHEST) + b_ref[...]


def _ada_mod(c, w_ada, b_ada):
    n_layers, d, n = w_ada.shape
    b = c.shape[0]
    rows = F32_SUBLANES
    c_pad = jnp.zeros((rows, d), F32).at[:b].set(c)
    tn = ADA_COLS
    out = pl.pallas_call(
        _ada_kernel,
        grid=(n_layers, n // tn),
        in_specs=[
            pl.BlockSpec((rows, d), lambda l, j: (0, 0)),
            pl.BlockSpec((None, d, tn), lambda l, j: (l, 0, j)),
            pl.BlockSpec((None, 1, tn), lambda l, j: (l, 0, j)),
        ],
        out_specs=pl.BlockSpec((None, rows, tn), lambda l, j: (l, 0, j)),
        out_shape=jax.ShapeDtypeStruct((n_layers, rows, n), F32),
        compiler_params=_params("parallel", "parallel"),
        name="ada_mod",
    )(c_pad, w_ada, b_ada.reshape(n_layers, 1, n))
    return out[:, :b].reshape(n_layers, b, N_MOD, 1, d)


def _norm_mod_kernel(x_ref, g_ref, sc_ref, sh_ref, o_ref):
    x = x_ref[...]
    y = x * lax.rsqrt(jnp.mean(x * x, axis=-1, keepdims=True) + EPS) * g_ref[...]
    o_ref[...] = (y * (1.0 + sc_ref[...]) + sh_ref[...]).astype(o_ref.dtype)


def _norm_mod(x2, g, mod, j_scale, j_shift, seq):
    t, d = x2.shape
    ts = min(NORM_ROWS, seq)
    per_b = seq // ts
    return pl.pallas_call(
        _norm_mod_kernel,
        grid=(t // ts,),
        in_specs=[
            pl.BlockSpec((ts, d), lambda i: (i, 0)),
            pl.BlockSpec((1, d), lambda i: (0, 0)),
            pl.BlockSpec((None, None, 1, d), lambda i: (i // per_b, j_scale, 0, 0)),
            pl.BlockSpec((None, None, 1, d), lambda i: (i // per_b, j_shift, 0, 0)),
        ],
        out_specs=pl.BlockSpec((ts, d), lambda i: (i, 0)),
        out_shape=jax.ShapeDtypeStruct((t, d), BF16),
        compiler_params=_params("parallel"),
        name="norm_mod",
    )(x2, g.reshape(1, d), mod, mod)


def _in_proj_kernel(h_ref, wt_ref, o_ref, w_scr, *, scaled_block, scale):
    m = pl.program_id(0)

    @pl.when(pl.program_id(1) == 0)
    def _():
        w = wt_ref[0] * jnp.where(m == scaled_block, scale, 1.0)
        w_scr[...] = w.T.astype(BF16)

    o_ref[...] = _dot(h_ref[...], w_scr[...]).astype(o_ref.dtype)


def _in_proj(h, w_in_t, layer, width, n_gate, scaled_block, scale):
    t, k = h.shape
    n_out = w_in_t.shape[1] - n_gate
    tm = min(PROJ_ROWS, t)
    tn = width
    aligned_blocks = 4
    row0 = lambda m: pl.multiple_of(m * tn + jnp.where(m >= aligned_blocks, n_gate, 0), n_gate)
    return pl.pallas_call(
        functools.partial(_in_proj_kernel, scaled_block=scaled_block, scale=scale),
        grid=(n_out // tn, t // tm),
        in_specs=[
            pl.BlockSpec((tm, k), lambda m, i: (i, 0)),
            pl.BlockSpec((pl.Element(1), pl.Element(tn), pl.Element(k)),
                         lambda m, i: (layer, row0(m), 0)),
        ],
        out_specs=pl.BlockSpec((tm, tn), lambda m, i: (i, m)),
        out_shape=jax.ShapeDtypeStruct((t, n_out), BF16),
        scratch_shapes=[pltpu.VMEM((k, tn), BF16)],
        compiler_params=_params("parallel", "arbitrary"),
        name="in_proj",
    )(h, w_in_t)


def _gate_proj_kernel(a_ref, wt_ref, b_ref, o_ref):
    wt = wt_ref[...]
    pad = jnp.zeros((o_ref.shape[1] - wt.shape[0], wt.shape[1]), wt.dtype)
    w = jnp.concatenate([wt, pad], axis=0).astype(BF16)
    o_ref[...] = _dot_nt(a_ref[...], w) + b_ref[...]


def _gate_proj(a, w_in_t, layer, gate0, bias, n_gate):
    t, k = a.shape
    tm = min(PROJ_ROWS, t)
    return pl.pallas_call(
        _gate_proj_kernel,
        grid=(t // tm,),
        in_specs=[
            pl.BlockSpec((tm, k), lambda i: (i, 0)),
            pl.BlockSpec((None, n_gate, k), lambda i: (layer, gate0 // n_gate, 0)),
            pl.BlockSpec((1, LANE), lambda i: (0, 0)),
        ],
        out_specs=pl.BlockSpec((tm, LANE), lambda i: (i, 0)),
        out_shape=jax.ShapeDtypeStruct((t, LANE), F32),
        compiler_params=_params("parallel"),
        name="gate_proj",
    )(a, w_in_t, bias)


def _mlstm_kernel(q_ref, k_ref, v_ref, o_ref, zif_ref, wconv_ref, g_ref, out_ref,
                  c_scr, n_scr, m_scr, qtail_scr, ktail_scr, shift_scr, *, dh):
    n_seq, chunk, width = q_ref.shape
    heads = width // dh
    halo = qtail_scr.shape[1]

    @pl.when(pl.program_id(1) == 0)
    def _():
        c_scr[...] = jnp.zeros_like(c_scr)
        n_scr[...] = jnp.zeros_like(n_scr)
        m_scr[...] = jnp.zeros_like(m_scr)
        qtail_scr[...] = jnp.zeros_like(qtail_scr)
        ktail_scr[...] = jnp.zeros_like(ktail_scr)
        srow = lax.broadcasted_iota(jnp.int32, shift_scr.shape, 0)
        scol = lax.broadcasted_iota(jnp.int32, shift_scr.shape, 1)
        delay = srow >> (chunk.bit_length() - 1)
        shift_scr[...] = (scol - halo == (srow & (chunk - 1)) - delay).astype(BF16)

    wconv = wconv_ref[...]
    shift = shift_scr[...]

    def conv_silu(x_ref, tail_scr, w, b):
        raw = x_ref[b]
        delayed = _dot(shift, jnp.concatenate([tail_scr[b], raw], axis=0))
        tail_scr[b] = raw[chunk - halo:]
        acc = delayed[:chunk] * w[M_CONV - 1:M_CONV]
        for back in range(1, M_CONV):
            acc = acc + delayed[back * chunk:(back + 1) * chunk] * w[M_CONV - 1 - back:M_CONV - back]
        return _silu(acc)

    row = lax.broadcasted_iota(jnp.int32, (chunk, chunk), 0)
    col = lax.broadcasted_iota(jnp.int32, (chunk, chunk), 1)
    causal = col <= row
    tri = causal.astype(BF16)
    tri3 = jnp.concatenate([tri, tri, tri], axis=1)

    for b in range(n_seq):
        q_all = conv_silu(q_ref, qtail_scr, wconv[:, :width], b) * (dh ** -0.5)
        k_all = conv_silu(k_ref, ktail_scr, wconv[:, width:], b)

        zif = zif_ref[b]
        lsf = _log_sigmoid(zif)
        p1 = lsf.astype(BF16)
        r1 = lsf - p1.astype(F32)
        p2 = r1.astype(BF16)
        p3 = (r1 - p2.astype(F32)).astype(BF16)
        b_all = _dot(tri3, jnp.concatenate([p1, p2, p3], axis=0))
        b_all_t = b_all.T
        zif_t = zif.T

        for h in range(heads):
            state = b * heads + h
            sl = slice(h * dh, (h + 1) * dh)
            qh = q_all[:, sl]
            kh = k_all[:, sl]
            qb = qh.astype(BF16)
            kb = kh.astype(BF16)
            vb = v_ref[b, :, sl]

            i_col = zif[:, h:h + 1]
            i_row = zif_t[h:h + 1, :]
            b_col = b_all[:, heads + h:heads + h + 1]
            b_row = b_all_t[heads + h:heads + h + 1, :]
            b_last = b_row[:, chunk - 1:chunk]

            m_prev = m_scr[state][:, :1]
            dmat = jnp.where(causal, b_col - b_row + i_row, NEG_INF)
            m_inter = b_col + m_prev
            m_t = jnp.maximum(jnp.max(dmat, axis=-1, keepdims=True), m_inter)
            w = _dot_nt(qb, kb) * jnp.exp(dmat - m_t)
            decay = jnp.exp(m_inter - m_t)
            c_prev = c_scr[state]
            n_prev = n_scr[state]
            num = _dot(w.astype(BF16), vb) + decay * _dot(qb, c_prev.astype(BF16))
            den = (jnp.sum(w, axis=-1, keepdims=True)
                   + decay * jnp.sum(qh * n_prev, axis=-1, keepdims=True))
            hh = num / jnp.maximum(jnp.abs(den), jnp.exp(-m_t))

            hn = hh * lax.rsqrt(jnp.mean(hh * hh, axis=-1, keepdims=True) + EPS) * g_ref[:, sl]
            gate = jax.nn.sigmoid(o_ref[b, :, sl].astype(F32))
            out_ref[b, :, sl] = (hn * gate).astype(out_ref.dtype)

            g_col = b_last - b_col + i_col
            m_new = jnp.maximum(b_last + m_prev, jnp.max(g_col, axis=0, keepdims=True))
            wk = jnp.exp(g_col - m_new)
            carry_decay = jnp.exp(b_last + m_prev - m_new)
            kw = kh * wk
            c_scr[state] = carry_decay * c_prev + _dot_tn(kw.astype(BF16), vb)
            n_scr[state] = carry_decay * n_prev + jnp.sum(kw, axis=0, keepdims=True)
            m_scr[state] = jnp.broadcast_to(m_new, m_scr.shape[1:])


def _mlstm(zmain, zif, w_mconv, g_mout, batch, seq, width):
    dh = width // M_HEADS
    nc = seq // M_CHUNK
    nb = MLSTM_SEQS_PER_STEP if batch % MLSTM_SEQS_PER_STEP == 0 else 1
    states = nb * M_HEADS
    z3 = zmain.reshape(batch, seq, zmain.shape[1])
    col_map = lambda cb: (lambda b, c: (b, c, cb))
    out = pl.pallas_call(
        functools.partial(_mlstm_kernel, dh=dh),
        grid=(batch // nb, nc),
        in_specs=[
            pl.BlockSpec((nb, M_CHUNK, width), col_map(0)),
            pl.BlockSpec((nb, M_CHUNK, width), col_map(1)),
            pl.BlockSpec((nb, M_CHUNK, width), col_map(2)),
            pl.BlockSpec((nb, M_CHUNK, width), col_map(3)),
            pl.BlockSpec((nb, M_CHUNK, LANE), col_map(0)),
            pl.BlockSpec((M_CONV, 2 * width), lambda b, c: (0, 0)),
            pl.BlockSpec((1, width), lambda b, c: (0, 0)),
        ],
        out_specs=pl.BlockSpec((nb, M_CHUNK, width), col_map(0)),
        out_shape=jax.ShapeDtypeStruct((batch, seq, width), BF16),
        scratch_shapes=[
            pltpu.VMEM((states, dh, dh), F32),
            pltpu.VMEM((states, 1, dh), F32),
            pltpu.VMEM((states, 1, LANE), F32),
            pltpu.VMEM((nb, BF16_SUBLANES, width), BF16),
            pltpu.VMEM((nb, BF16_SUBLANES, width), BF16),
            pltpu.VMEM((M_CONV * M_CHUNK, BF16_SUBLANES + M_CHUNK), BF16),
        ],
        compiler_params=_params("parallel", "arbitrary"),
        name="mlstm",
    )(z3, z3, z3, z3, zif.reshape(batch, seq, LANE), w_mconv, g_mout.reshape(1, width))
    return out.reshape(batch * seq, width)


def _qk_norm_kernel(q_ref, k_ref, gq_ref, gk_ref, qo_ref, ko_ref, *, dh):
    width = q_ref.shape[1]
    row = lax.broadcasted_iota(jnp.int32, (LANE, LANE), 0)
    col = lax.broadcasted_iota(jnp.int32, (LANE, LANE), 1)
    same_group = ((row // dh) == (col // dh)).astype(BF16)

    def group_rms(x_ref, g_ref, o_ref, scale):
        for tile in range(width // LANE):
            sl = slice(tile * LANE, (tile + 1) * LANE)
            x = x_ref[:, sl].astype(F32)
            sq = x * x
            hi = sq.astype(BF16)
            lo = (sq - hi.astype(F32)).astype(BF16)
            ssum = _dot(hi, same_group) + _dot(lo, same_group)
            y = x * lax.rsqrt(ssum * (1.0 / dh) + EPS) * g_ref[:, sl]
            o_ref[:, sl] = (y * scale).astype(o_ref.dtype)

    group_rms(q_ref, gq_ref, qo_ref, dh ** -0.5 * LOG2E)
    group_rms(k_ref, gk_ref, ko_ref, 1.0)


def _qk_norm(zmain, g_dq, g_dk, width, col_q, col_k, seq):
    t = zmain.shape[0]
    dh = g_dq.shape[0]
    ts = min(NORM_ROWS, seq)
    reps = width // dh
    return pl.pallas_call(
        functools.partial(_qk_norm_kernel, dh=dh),
        grid=(t // ts,),
        in_specs=[
            pl.BlockSpec((ts, width), lambda i: (i, col_q)),
            pl.BlockSpec((ts, width), lambda i: (i, col_k)),
            pl.BlockSpec((1, width), lambda i: (0, 0)),
            pl.BlockSpec((1, width), lambda i: (0, 0)),
        ],
        out_specs=[pl.BlockSpec((ts, width), lambda i: (i, 0)),
                   pl.BlockSpec((ts, width), lambda i: (i, 0))],
        out_shape=[jax.ShapeDtypeStruct((t, width), BF16), jax.ShapeDtypeStruct((t, width), BF16)],
        compiler_params=_params("parallel"),
        name="qk_norm",
    )(zmain, zmain, jnp.tile(g_dq, reps).reshape(1, width), jnp.tile(g_dk, reps).reshape(1, width))


def _diff_attn_kernel(q_ref, k_ref, v_ref, lq1_ref, lk1_ref, lq2_ref, lk2_ref, g_ref, out_ref,
                      m_scr, l_scr, acc_scr, s_scr, *, lam_init, dh):
    tq = q_ref.shape[0]
    hw = 2 * dh
    heads = q_ref.shape[1] // hw
    i = pl.program_id(2)
    head_lanes = [slice(h * hw, (h + 1) * hw) for h in range(heads)]

    q_streams = []
    for lanes in head_lanes:
        q = q_ref[:, lanes]
        lane = lax.broadcasted_iota(jnp.int32, q.shape, 1)
        zero = jnp.zeros_like(q)
        q_streams += [jnp.where(lane < dh, q, zero), jnp.where(lane >= dh, q, zero)]

    m_scr[...] = jnp.full_like(m_scr, NEG_INF)
    l_scr[...] = jnp.zeros_like(l_scr)
    acc_scr[...] = jnp.zeros_like(acc_scr)

    def key_block(ref, j, lanes):
        return ref[pl.ds(pl.multiple_of(j * tq, tq), tq), lanes]

    def put_scores(j, slot):
        for h, lanes in enumerate(head_lanes):
            k = key_block(k_ref, j, lanes)
            for c in range(2):
                s_scr[slot, 2 * h + c] = _dot_nt(k, q_streams[2 * h + c])

    def consume(j, slot, masked):
        probs, alphas = [], []
        for u in range(2 * heads):
            s = s_scr[slot, u]
            if masked:
                key = lax.broadcasted_iota(jnp.int32, s.shape, 0)
                qry = lax.broadcasted_iota(jnp.int32, s.shape, 1)
                s = jnp.where(key <= qry, s, NEG_INF)
            m_prev = m_scr[u]
            m_new = jnp.maximum(m_prev, jnp.max(s, axis=0, keepdims=True))
            alpha = jnp.exp2(m_prev - m_new)
            p = jnp.exp2(s - m_new)
            l_scr[u] = alpha * l_scr[u] + jnp.sum(p, axis=0, keepdims=True)
            m_scr[u] = m_new
            probs.append(p.astype(BF16))
            alphas.append(alpha)
        for h, lanes in enumerate(head_lanes):
            v = key_block(v_ref, j, lanes)
            for u in (2 * h, 2 * h + 1):
                acc_scr[u] = alphas[u] * acc_scr[u] + _dot_tn(v, probs[u])

    put_scores(0, 0)

    def pair(p, carry):
        j = 2 * p
        put_scores(j + 1, 1)
        consume(j, 0, False)
        put_scores(j + 2, 0)
        consume(j + 1, 1, False)
        return carry

    lax.fori_loop(0, i // 2, pair, 0)

    @pl.when(i % 2 == 0)
    def _():
        consume(i, 0, True)

    @pl.when(i % 2 == 1)
    def _():
        put_scores(i, 1)
        consume(i - 1, 0, False)
        consume(i, 1, True)

    lam = (jnp.exp(jnp.sum(lq1_ref[...] * lk1_ref[...], axis=-1, keepdims=True))
           - jnp.exp(jnp.sum(lq2_ref[...] * lk2_ref[...], axis=-1, keepdims=True)) + lam_init)
    for h, lanes in enumerate(head_lanes):
        u = 2 * h
        o = (acc_scr[u] / l_scr[u] - lam * (acc_scr[u + 1] / l_scr[u + 1])).T
        on = o * lax.rsqrt(jnp.mean(o * o, axis=-1, keepdims=True) + EPS) * g_ref[...]
        out_ref[:, lanes] = (on * (1.0 - lam_init)).astype(out_ref.dtype)


def _diff_attn(qn, kn, zmain, col_v, lam_params, g_dsub, lam_init, batch, seq, width):
    t = qn.shape[0]
    dh = lam_params[0].shape[0]
    hw = 2 * dh
    hb = ATTN_HEADS_PER_STEP
    bw = hb * hw
    tq = min(ATTN_ROWS, seq)
    nq = seq // tq
    v_col0 = col_v * (width // bw)
    lam_specs = [pl.BlockSpec((1, dh), lambda b, h, i: (0, 0)) for _ in range(4)]
    return pl.pallas_call(
        functools.partial(_diff_attn_kernel, lam_init=lam_init, dh=dh),
        grid=(batch, width // bw, nq),
        in_specs=[
            pl.BlockSpec((tq, bw), lambda b, h, i: (b * nq + i, h)),
            pl.BlockSpec((seq, bw), lambda b, h, i: (b, h)),
            pl.BlockSpec((seq, bw), lambda b, h, i: (b, v_col0 + h)),
            *lam_specs,
            pl.BlockSpec((1, hw), lambda b, h, i: (0, 0)),
        ],
        out_specs=pl.BlockSpec((tq, bw), lambda b, h, i: (b * nq + i, h)),
        out_shape=jax.ShapeDtypeStruct((t, width), BF16),
        scratch_shapes=[
            pltpu.VMEM((2 * hb, 1, tq), F32),
            pltpu.VMEM((2 * hb, 1, tq), F32),
            pltpu.VMEM((2 * hb, hw, tq), F32),
            pltpu.VMEM((2, 2 * hb, tq, tq), F32),
        ],
        compiler_params=_params("parallel", "parallel", "arbitrary"),
        name="diff_attn",
    )(qn, kn, zmain, *[p.reshape(1, dh) for p in lam_params], g_dsub.reshape(1, hw))


def _stick_kernel(q_ref, k_ref, v_ref, out_ref, carry_scr, acc_scr, *, dh, sub, far):
    tq = q_ref.shape[0]
    heads = q_ref.shape[1] // dh
    i = pl.program_id(2)
    sign_bit = jnp.uint32(1 << 31)

    srow = lax.broadcasted_iota(jnp.int32, (sub, 2 * sub), 0)
    jcol = lax.broadcasted_iota(jnp.int32, (sub, 2 * sub), 1)
    later = ((jcol & (sub - 1)) > srow).astype(BF16)

    def softplus2(z2):
        neg_abs = pltpu.bitcast(pltpu.bitcast(z2, jnp.uint32) | sign_bit, F32)
        return jnp.maximum(z2, 0.0) + jnp.log(1.0 + jnp.exp2(neg_abs)) * LOG2E

    def suffix_in_block(sp2):
        hi = sp2.astype(BF16)
        lo = (sp2 - hi.astype(F32)).astype(BF16)
        return _dot(later, jnp.concatenate([hi, lo], axis=0))

    def own_keys(q, lanes):
        start = pl.multiple_of(i * tq, tq)
        z_tile = _dot_nt(k_ref[pl.ds(start, tq), lanes], q)
        carry = jnp.zeros((1, tq), F32)
        key_rows = []
        for c in reversed(range(tq // sub)):
            lane0 = c * sub
            z2 = z_tile[lane0:lane0 + sub, lane0:]
            before = (lax.broadcasted_iota(jnp.int32, z2.shape, 0)
                      < lax.broadcasted_iota(jnp.int32, z2.shape, 1))
            sp2 = jnp.where(before, softplus2(z2), 0.0)
            a = jnp.exp2(z2 - sp2 - (suffix_in_block(sp2) + carry[:, lane0:]))
            a = jnp.where(before, a, 0.0).astype(BF16)
            total = jnp.sum(sp2, axis=0, keepdims=True)
            if lane0:
                a = jnp.concatenate([jnp.zeros((sub, lane0), BF16), a], axis=1)
                total = jnp.concatenate([jnp.zeros((1, lane0), F32), total], axis=1)
            key_rows.insert(0, a)
            carry = carry + total
        return _dot_tn(v_ref[pl.ds(start, tq), lanes], jnp.concatenate(key_rows, axis=0)), carry

    def far_keys(q, lanes, jb, carry, valid):
        start = pl.multiple_of(jb * far, far)
        z2 = _dot_nt(k_ref[pl.ds(start, far), lanes], q)
        sp2 = softplus2(z2)
        if valid is not None:
            sp2 = jnp.where(valid, sp2, 0.0)
        parts = []
        for c in reversed(range(far // sub)):
            rows = slice(c * sub, (c + 1) * sub)
            a = jnp.exp2(z2[rows] - sp2[rows] - (suffix_in_block(sp2[rows]) + carry))
            if valid is not None:
                a = jnp.where(valid, a, 0.0)
            parts.insert(0, a.astype(BF16))
            carry = carry + jnp.sum(sp2[rows], axis=0, keepdims=True)
        return _dot_tn(v_ref[pl.ds(start, far), lanes], jnp.concatenate(parts, axis=0)), carry

    jb = i * (tq // far) - 1
    min_carry = []
    for h in range(heads):
        lanes = slice(h * dh, (h + 1) * dh)
        q = q_ref[:, lanes]
        acc, carry = own_keys(q, lanes)
        near, carry = far_keys(q, lanes, jnp.maximum(jb, 0), carry, i > 0)
        acc_scr[h] = acc + near
        carry_scr[h] = carry
        min_carry.append(jnp.min(carry))

    def keep_going(state):
        jb, min_carry = state
        return jnp.logical_and(jb >= 0, min_carry <= STICK_DEAD_LOG2)

    for h in range(heads):
        lanes = slice(h * dh, (h + 1) * dh)

        def body(state, h=h, lanes=lanes):
            jb, _ = state
            more, carry = far_keys(q_ref[:, lanes], lanes, jb, carry_scr[h], None)
            acc_scr[h] += more
            carry_scr[h] = carry
            return jb - 1, jnp.min(carry)

        lax.while_loop(keep_going, body, (jb - 1, min_carry[h]))
        out_ref[:, lanes] = acc_scr[h].T.astype(out_ref.dtype)


def _stick_breaking(zmain, col_q, col_k, col_v, batch, seq, width):
    t = zmain.shape[0]
    dh = width // SB_HEADS
    hb = ATTN_HEADS_PER_STEP
    tq = min(ATTN_ROWS, seq)
    nq = seq // tq
    per = width // (hb * dh)
    return pl.pallas_call(
        functools.partial(_stick_kernel, dh=dh, sub=LANE, far=min(STICK_FAR_KEYS, seq)),
        grid=(batch, SB_HEADS // hb, nq),
        in_specs=[
            pl.BlockSpec((tq, hb * dh), lambda b, h, i: (b * nq + i, col_q * per + h)),
            pl.BlockSpec((seq, hb * dh), lambda b, h, i: (b, col_k * per + h)),
            pl.BlockSpec((seq, hb * dh), lambda b, h, i: (b, col_v * per + h)),
        ],
        out_specs=pl.BlockSpec((tq, hb * dh), lambda b, h, i: (b * nq + i, h)),
        out_shape=jax.ShapeDtypeStruct((t, width), BF16),
        scratch_shapes=[
            pltpu.VMEM((hb, 1, tq), F32),
            pltpu.VMEM((hb, dh, tq), F32),
        ],
        compiler_params=_params("parallel", "parallel", "arbitrary"),
        name="stick_breaking",
    )(zmain, zmain, zmain)


def _merge_kernel(om_ref, od_ref, os_ref, wm_ref, wd_ref, ws_ref, gm_ref, gd_ref, gs_ref, o_ref, w_scr):
    @pl.when(pl.program_id(1) == 0)
    def _():
        for br, w_ref in enumerate((wm_ref, wd_ref, ws_ref)):
            w_scr[br] = w_ref[...].astype(BF16)

    acc = jax.nn.sigmoid(gm_ref[...].astype(F32)) * _dot(om_ref[...], w_scr[0])
    acc = acc + jax.nn.sigmoid(gd_ref[...].astype(F32)) * _dot(od_ref[...], w_scr[1])
    acc = acc + jax.nn.sigmoid(gs_ref[...].astype(F32)) * _dot(os_ref[...], w_scr[2])
    o_ref[...] = acc.astype(o_ref.dtype)


def _merge(out_m, out_d, out_s, w_branch, layer, zmain, gate_col0, d_model):
    t, width = out_m.shape
    tm = min(PROJ_ROWS, t)
    tn = HALF_COLS
    nj = d_model // tn
    act_spec = pl.BlockSpec((tm, width), lambda j, i: (i, 0))
    w_spec = lambda br: pl.BlockSpec((None, None, width, tn), lambda j, i: (layer, br, 0, j))
    gate_blk0 = gate_col0 // tn
    gate_spec = lambda br: pl.BlockSpec((tm, tn), lambda j, i: (i, gate_blk0 + br * nj + j))
    return pl.pallas_call(
        _merge_kernel,
        grid=(nj, t // tm),
        in_specs=[act_spec, act_spec, act_spec, w_spec(0), w_spec(1), w_spec(2),
                  gate_spec(0), gate_spec(1), gate_spec(2)],
        out_specs=pl.BlockSpec((tm, tn), lambda j, i: (i, j)),
        out_shape=jax.ShapeDtypeStruct((t, d_model), BF16),
        scratch_shapes=[pltpu.VMEM((3, width, tn), BF16)],
        compiler_params=_params("parallel", "arbitrary"),
        name="branch_merge",
    )(out_m, out_d, out_s, w_branch, w_branch, w_branch, zmain, zmain, zmain)


def _proj_residual_kernel(a_ref, w_ref, x_ref, gt_ref, o_ref, w_scr):
    @pl.when(pl.program_id(1) == 0)
    def _():
        w_scr[...] = w_ref[...].astype(BF16)

    o_ref[...] = x_ref[...] + gt_ref[...] * _dot(a_ref[...], w_scr[...])


def _proj_residual(a, w, layer, x2, mod, j_gate, seq, tm, tn, name):
    t, k = a.shape
    n = w.shape[2]
    tm = min(tm, seq)
    per_b = seq // tm
    return pl.pallas_call(
        _proj_residual_kernel,
        grid=(n // tn, t // tm),
        in_specs=[
            pl.BlockSpec((tm, k), lambda j, i: (i, 0)),
            pl.BlockSpec((None, k, tn), lambda j, i: (layer, 0, j)),
            pl.BlockSpec((tm, tn), lambda j, i: (i, j)),
            pl.BlockSpec((None, None, 1, tn), lambda j, i: (i // per_b, j_gate, 0, j)),
        ],
        out_specs=pl.BlockSpec((tm, tn), lambda j, i: (i, j)),
        out_shape=jax.ShapeDtypeStruct((t, n), F32),
        scratch_shapes=[pltpu.VMEM((k, tn), BF16)],
        compiler_params=_params("parallel", "arbitrary"),
        name=name,
    )(a, w, x2, mod)


def _ffn_up_kernel(h_ref, wg_ref, wu_ref, wc_ref, o_ref, tail_scr, wg_scr, wu_scr, *, tiles_per_seq):
    tm = h_ref.shape[0]
    i = pl.program_id(1)

    @pl.when(i == 0)
    def _():
        wg_scr[...] = wg_ref[...].astype(BF16)
        wu_scr[...] = wu_ref[...].astype(BF16)

    @pl.when(i % tiles_per_seq == 0)
    def _():
        tail_scr[...] = jnp.zeros_like(tail_scr)

    h = h_ref[...]
    gate = _dot(h, wg_scr[...])
    up = _dot(h, wu_scr[...])
    wc = wc_ref[...]
    xp = jnp.concatenate([tail_scr[...], gate], axis=0)
    conv = gate * wc[FF_CONV - 1:FF_CONV]
    for back in range(1, FF_CONV):
        conv = conv + pltpu.roll(xp, back, 0)[CONV_HALO:] * wc[FF_CONV - 1 - back:FF_CONV - back]
    tail_scr[...] = gate[tm - CONV_HALO:]
    o_ref[...] = (_silu(conv) * up).astype(o_ref.dtype)


def _ffn_up(h2, w_up, w_ffconv, layer, seq):
    t, k = h2.shape
    d_ff = w_ffconv.shape[2]
    tm = min(PROJ_ROWS, seq)
    tn = HALF_COLS
    nj = d_ff // tn
    return pl.pallas_call(
        functools.partial(_ffn_up_kernel, tiles_per_seq=seq // tm),
        grid=(nj, t // tm),
        in_specs=[
            pl.BlockSpec((tm, k), lambda j, i: (i, 0)),
            pl.BlockSpec((None, k, tn), lambda j, i: (layer, 0, j)),
            pl.BlockSpec((None, k, tn), lambda j, i: (layer, 0, nj + j)),
            pl.BlockSpec((None, FF_CONV, tn), lambda j, i: (layer, 0, j)),
        ],
        out_specs=pl.BlockSpec((tm, tn), lambda j, i: (i, j)),
        out_shape=jax.ShapeDtypeStruct((t, d_ff), BF16),
        scratch_shapes=[pltpu.VMEM((CONV_HALO, tn), F32), pltpu.VMEM((k, tn), BF16),
                        pltpu.VMEM((k, tn), BF16)],
        compiler_params=_params("parallel", "arbitrary"),
        name="ffn_up",
    )(h2, w_up, w_up, w_ffconv)


def kernel(x, c, w_ada, b_ada, g_mix, g_ffn, w_in, b_gate_if, w_mconv, g_mout, g_dq, g_dk,
           lam_q1, lam_k1, lam_q2, lam_k2, g_dsub, w_branch, w_out, w_up, w_ffconv, w_down):
    batch, seq, d = x.shape
    depth = w_ada.shape[0]
    width = d // 2
    t = batch * seq
    n_gate = 2 * M_HEADS

    mod_all = _ada_mod(c, w_ada, b_ada)
    x2 = x.reshape(t, d)
    w_in_t = jnp.swapaxes(w_in, 1, 2)

    for l in range(depth):
        mod = mod_all[l]
        gate0 = 4 * width
        sb_scale = (width // SB_HEADS) ** -0.5 * LOG2E
        b_if = jnp.zeros((1, LANE), F32).at[0, :n_gate].set(b_gate_if[l])

        h = _norm_mod(x2, g_mix[l], mod, 1, 0, seq)
        zmain = _in_proj(h, w_in_t, l, width, n_gate, 7, sb_scale)
        zif = _gate_proj(h, w_in_t, l, gate0, b_if, n_gate)

        out_m = _mlstm(zmain, zif, w_mconv[l], g_mout[l], batch, seq, width)
        qn, kn = _qk_norm(zmain, g_dq[l], g_dk[l], width, 4, 5, seq)
        lam_init = 0.8 - 0.6 * math.exp(-0.3 * l)
        out_d = _diff_attn(qn, kn, zmain, 6, (lam_q1[l], lam_k1[l], lam_q2[l], lam_k2[l]),
                           g_dsub[l], lam_init, batch, seq, width)
        out_s = _stick_breaking(zmain, 7, 8, 9, batch, seq, width)

        merged = _merge(out_m, out_d, out_s, w_branch, l, zmain, 10 * width, d)
        x2 = _proj_residual(merged, w_out, l, x2, mod, 2, seq, PROJ_ROWS, PROJ_COLS, "out_proj")

        h2 = _norm_mod(x2, g_ffn[l], mod, 4, 3, seq)
        act = _ffn_up(h2, w_up, w_ffconv, l, seq)
        x2 = _proj_residual(act, w_down, l, x2, mod, 5, seq, FFN_DOWN_ROWS, HALF_COLS, "ffn_down")

    return x2.reshape(batch, seq, d)
```

```python
import functools
import math

import jax
import jax.numpy as jnp
from jax import lax
from jax.experimental import pallas as pl
from jax.experimental.pallas import tpu as pltpu

EPS = 1e-6
M_HEADS = 4
M_CONV = 4
M_CHUNK = 128
DA_HEADS = 8
SB_HEADS = 8
N_MOD = 6
FF_CONV = 3
LANE = 128
F32_SUBLANES = 8
BF16_SUBLANES = 16
CONV_HALO = F32_SUBLANES

ADA_COLS = 1024
NORM_ROWS = 512
PROJ_ROWS = 1024
PROJ_COLS = 1024
HALF_COLS = 512
FFN_DOWN_ROWS = 512
ATTN_ROWS = 512
ATTN_HEADS_PER_STEP = 4
STICK_FAR_KEYS = 256
MLSTM_SEQS_PER_STEP = 2
VMEM_LIMIT_BYTES = 56 * 1024 * 1024
HIGHEST = lax.Precision.HIGHEST
NEG_INF = float("-inf")
LOG2E = math.log2(math.e)
STICK_DEAD_LOG2 = 152.0

BF16 = jnp.bfloat16
F32 = jnp.float32


def _params(*sem):
    return pltpu.CompilerParams(dimension_semantics=sem, vmem_limit_bytes=VMEM_LIMIT_BYTES)


def _dot(a, b):
    return jnp.dot(a, b, preferred_element_type=F32)


def _dot_nt(a, b):
    return lax.dot_general(a, b, (((1,), (1,)), ((), ())), preferred_element_type=F32)


def _dot_tn(a, b):
    return lax.dot_general(a, b, (((0,), (0,)), ((), ())), preferred_element_type=F32)


def _silu(x):
    return x * jax.nn.sigmoid(x)


def _softplus(x):
    return jnp.maximum(x, 0.0) + jnp.log1p(jnp.exp(-jnp.abs(x)))


def _log_sigmoid(x):
    return -_softplus(-x)


def _ada_kernel(c_ref, w_ref, b_ref, o_ref):
    c = c_ref[...]
    o_ref[...] = jnp.dot(_silu(c), w_ref[...], preferred_element_type=F32,
                         precision=HIGHEST) + b_ref[...]


def _ada_mod(c, w_ada, b_ada):
    n_layers, d, n = w_ada.shape
    b = c.shape[0]
    rows = F32_SUBLANES
    c_pad = jnp.zeros((rows, d), F32).at[:b].set(c)
    tn = ADA_COLS
    out = pl.pallas_call(
        _ada_kernel,
        grid=(n_layers, n // tn),
        in_specs=[
            pl.BlockSpec((rows, d), lambda l, j: (0, 0)),
            pl.BlockSpec((None, d, tn), lambda l, j: (l, 0, j)),
            pl.BlockSpec((None, 1, tn), lambda l, j: (l, 0, j)),
        ],
        out_specs=pl.BlockSpec((None, rows, tn), lambda l, j: (l, 0, j)),
        out_shape=jax.ShapeDtypeStruct((n_layers, rows, n), F32),
        compiler_params=_params("parallel", "parallel"),
        name="ada_mod",
    )(c_pad, w_ada, b_ada.reshape(n_layers, 1, n))
    return out[:, :b].reshape(n_layers, b, N_MOD, 1, d)


def _norm_mod_kernel(x_ref, g_ref, sc_ref, sh_ref, o_ref):
    x = x_ref[...]
    y = x * lax.rsqrt(jnp.mean(x * x, axis=-1, keepdims=True) + EPS) * g_ref[...]
    o_ref[...] = (y * (1.0 + sc_ref[...]) + sh_ref[...]).astype(o_ref.dtype)


def _norm_mod(x2, g, mod, j_scale, j_shift, seq):
    t, d = x2.shape
    ts = min(NORM_ROWS, seq)
    per_b = seq // ts
    return pl.pallas_call(
        _norm_mod_kernel,
        grid=(t // ts,),
        in_specs=[
            pl.BlockSpec((ts, d), lambda i: (i, 0)),
            pl.BlockSpec((1, d), lambda i: (0, 0)),
            pl.BlockSpec((None, None, 1, d), lambda i: (i // per_b, j_scale, 0, 0)),
            pl.BlockSpec((None, None, 1, d), lambda i: (i // per_b, j_shift, 0, 0)),
        ],
        out_specs=pl.BlockSpec((ts, d), lambda i: (i, 0)),
        out_shape=jax.ShapeDtypeStruct((t, d), BF16),
        compiler_params=_params("parallel"),
        name="norm_mod",
    )(x2, g.reshape(1, d), mod, mod)


def _norm_mod_gates_kernel(x_ref, g_ref, sc_ref, sh_ref, wt_ref, b_ref, o_ref, zif_ref):
    x = x_ref[...]
    y = x * lax.rsqrt(jnp.mean(x * x, axis=-1, keepdims=True) + EPS) * g_ref[...]
    h = (y * (1.0 + sc_ref[...]) + sh_ref[...]).astype(o_ref.dtype)
    o_ref[...] = h
    wt = wt_ref[...]
    pad = jnp.zeros((zif_ref.shape[1] - wt.shape[0], wt.shape[1]), wt.dtype)
    w = jnp.concatenate([wt, pad], axis=0).astype(BF16)
    zif_ref[...] = _dot_nt(h, w) + b_ref[...]


def _norm_mod_gates(x2, g, mod, j_scale, j_shift, seq, w_in_t, layer, gate0, bias, n_gate):
    t, d = x2.shape
    ts = min(NORM_ROWS, seq)
    per_b = seq // ts
    return pl.pallas_call(
        _norm_mod_gates_kernel,
        grid=(t // ts,),
        in_specs=[
            pl.BlockSpec((ts, d), lambda i: (i, 0)),
            pl.BlockSpec((1, d), lambda i: (0, 0)),
            pl.BlockSpec((None, None, 1, d), lambda i: (i // per_b, j_scale, 0, 0)),
            pl.BlockSpec((None, None, 1, d), lambda i: (i // per_b, j_shift, 0, 0)),
            pl.BlockSpec((None, n_gate, d), lambda i: (layer, gate0 // n_gate, 0)),
            pl.BlockSpec((1, LANE), lambda i: (0, 0)),
        ],
        out_specs=[pl.BlockSpec((ts, d), lambda i: (i, 0)), pl.BlockSpec((ts, LANE), lambda i: (i, 0))],
        out_shape=[jax.ShapeDtypeStruct((t, d), BF16), jax.ShapeDtypeStruct((t, LANE), F32)],
        compiler_params=_params("parallel"),
        name="norm_mod_gates",
    )(x2, g.reshape(1, d), mod, mod, w_in_t, bias)


def _in_proj_kernel(h_ref, wt_ref, o_ref, w_scr, *, scaled_block, scale):
    m = pl.program_id(0)

    @pl.when(pl.program_id(1) == 0)
    def _():
        w = wt_ref[0] * jnp.where(m == scaled_block, scale, 1.0)
        w_scr[...] = w.T.astype(BF16)

    o_ref[...] = _dot(h_ref[...], w_scr[...]).astype(o_ref.dtype)


def _in_proj(h, w_in_t, layer, width, n_gate, scaled_block, scale):
    t, k = h.shape
    n_out = w_in_t.shape[1] - n_gate
    tm = min(PROJ_ROWS, t)
    tn = width
    aligned_blocks = 4
    row0 = lambda m: pl.multiple_of(m * tn + jnp.where(m >= aligned_blocks, n_gate, 0), n_gate)
    return pl.pallas_call(
        functools.partial(_in_proj_kernel, scaled_block=scaled_block, scale=scale),
        grid=(n_out // tn, t // tm),
        in_specs=[
            pl.BlockSpec((tm, k), lambda m, i: (i, 0)),
            pl.BlockSpec((pl.Element(1), pl.Element(tn), pl.Element(k)),
                         lambda m, i: (layer, row0(m), 0)),
        ],
        out_specs=pl.BlockSpec((tm, tn), lambda m, i: (i, m)),
        out_shape=jax.ShapeDtypeStruct((t, n_out), BF16),
        scratch_shapes=[pltpu.VMEM((k, tn), BF16)],
        compiler_params=_params("parallel", "arbitrary"),
        name="in_proj",
    )(h, w_in_t)


def _mlstm_kernel(q_ref, k_ref, v_ref, o_ref, zif_ref, wconv_ref, g_ref, out_ref,
                  c_scr, n_scr, m_scr, qtail_scr, ktail_scr, shift_scr, *, dh):
    n_seq, chunk, width = q_ref.shape
    heads = width // dh
    halo = qtail_scr.shape[1]

    @pl.when(pl.program_id(1) == 0)
    def _():
        c_scr[...] = jnp.zeros_like(c_scr)
        n_scr[...] = jnp.zeros_like(n_scr)
        m_scr[...] = jnp.zeros_like(m_scr)
        qtail_scr[...] = jnp.zeros_like(qtail_scr)
        ktail_scr[...] = jnp.zeros_like(ktail_scr)
        srow = lax.broadcasted_iota(jnp.int32, shift_scr.shape, 0)
        scol = lax.broadcasted_iota(jnp.int32, shift_scr.shape, 1)
        delay = srow >> (chunk.bit_length() - 1)
        shift_scr[...] = (scol - halo == (srow & (chunk - 1)) - delay).astype(BF16)

    wconv = wconv_ref[...]
    shift = shift_scr[...]

    def conv_silu(x_ref, tail_scr, w, b):
        raw = x_ref[b]
        delayed = _dot(shift, jnp.concatenate([tail_scr[b], raw], axis=0))
        tail_scr[b] = raw[chunk - halo:]
        acc = delayed[:chunk] * w[M_CONV - 1:M_CONV]
        for back in range(1, M_CONV):
            acc = acc + delayed[back * chunk:(back + 1) * chunk] * w[M_CONV - 1 - back:M_CONV - back]
        return _silu(acc)

    row = lax.broadcasted_iota(jnp.int32, (chunk, chunk), 0)
    col = lax.broadcasted_iota(jnp.int32, (chunk, chunk), 1)
    causal = col <= row
    tri = causal.astype(BF16)
    tri3 = jnp.concatenate([tri, tri, tri], axis=1)

    for b in range(n_seq):
        q_all = conv_silu(q_ref, qtail_scr, wconv[:, :width], b) * (dh ** -0.5)
        k_all = conv_silu(k_ref, ktail_scr, wconv[:, width:], b)

        zif = zif_ref[b]
        lsf = _log_sigmoid(zif)
        p1 = lsf.astype(BF16)
        r1 = lsf - p1.astype(F32)
        p2 = r1.astype(BF16)
        p3 = (r1 - p2.astype(F32)).astype(BF16)
        b_all = _dot(tri3, jnp.concatenate([p1, p2, p3], axis=0))
        b_all_t = b_all.T
        zif_t = zif.T

        for h in range(heads):
            state = b * heads + h
            sl = slice(h * dh, (h + 1) * dh)
            qh = q_all[:, sl]
            kh = k_all[:, sl]
            qb = qh.astype(BF16)
            kb = kh.astype(BF16)
            vb = v_ref[b, :, sl]

            i_col = zif[:, h:h + 1]
            i_row = zif_t[h:h + 1, :]
            b_col = b_all[:, heads + h:heads + h + 1]
            b_row = b_all_t[heads + h:heads + h + 1, :]
            b_last = b_row[:, chunk - 1:chunk]

            m_prev = m_scr[state][:, :1]
            dmat = jnp.where(causal, b_col - b_row + i_row, NEG_INF)
            m_inter = b_col + m_prev
            m_t = jnp.maximum(jnp.max(dmat, axis=-1, keepdims=True), m_inter)
            w = _dot_nt(qb, kb) * jnp.exp(dmat - m_t)
            decay = jnp.exp(m_inter - m_t)
            c_prev = c_scr[state]
            n_prev = n_scr[state]
            num = _dot(w.astype(BF16), vb) + decay * _dot(qb, c_prev.astype(BF16))
            den = (jnp.sum(w, axis=-1, keepdims=True)
                   + decay * jnp.sum(qh * n_prev, axis=-1, keepdims=True))
            hh = num / jnp.maximum(jnp.abs(den), jnp.exp(-m_t))

            hn = hh * lax.rsqrt(jnp.mean(hh * hh, axis=-1, keepdims=True) + EPS) * g_ref[:, sl]
            gate = jax.nn.sigmoid(o_ref[b, :, sl].astype(F32))
            out_ref[b, :, sl] = (hn * gate).astype(out_ref.dtype)

            g_col = b_last - b_col + i_col
            m_new = jnp.maximum(b_last + m_prev, jnp.max(g_col, axis=0, keepdims=True))
            wk = jnp.exp(g_col - m_new)
            carry_decay = jnp.exp(b_last + m_prev - m_new)
            kw = kh * wk
            c_scr[state] = carry_decay * c_prev + _dot_tn(kw.astype(BF16), vb)
            n_scr[state] = carry_decay * n_prev + jnp.sum(kw, axis=0, keepdims=True)
            m_scr[state] = jnp.broadcast_to(m_new, m_scr.shape[1:])


def _mlstm(zmain, zif, w_mconv, g_mout, batch, seq, width):
    dh = width // M_HEADS
    nc = seq // M_CHUNK
    nb = MLSTM_SEQS_PER_STEP if batch % MLSTM_SEQS_PER_STEP == 0 else 1
    states = nb * M_HEADS
    z3 = zmain.reshape(batch, seq, zmain.shape[1])
    col_map = lambda cb: (lambda b, c: (b, c, cb))
    out = pl.pallas_call(
        functools.partial(_mlstm_kernel, dh=dh),
        grid=(batch // nb, nc),
        in_specs=[
            pl.BlockSpec((nb, M_CHUNK, width), col_map(0)),
            pl.BlockSpec((nb, M_CHUNK, width), col_map(1)),
            pl.BlockSpec((nb, M_CHUNK, width), col_map(2)),
            pl.BlockSpec((nb, M_CHUNK, width), col_map(3)),
            pl.BlockSpec((nb, M_CHUNK, LANE), col_map(0)),
            pl.BlockSpec((M_CONV, 2 * width), lambda b, c: (0, 0)),
            pl.BlockSpec((1, width), lambda b, c: (0, 0)),
        ],
        out_specs=pl.BlockSpec((nb, M_CHUNK, width), col_map(0)),
        out_shape=jax.ShapeDtypeStruct((batch, seq, width), BF16),
        scratch_shapes=[
            pltpu.VMEM((states, dh, dh), F32),
            pltpu.VMEM((states, 1, dh), F32),
            pltpu.VMEM((states, 1, LANE), F32),
            pltpu.VMEM((nb, BF16_SUBLANES, width), BF16),
            pltpu.VMEM((nb, BF16_SUBLANES, width), BF16),
            pltpu.VMEM((M_CONV * M_CHUNK, BF16_SUBLANES + M_CHUNK), BF16),
        ],
        compiler_params=_params("parallel", "arbitrary"),
        name="mlstm",
    )(z3, z3, z3, z3, zif.reshape(batch, seq, LANE), w_mconv, g_mout.reshape(1, width))
    return out.reshape(batch * seq, width)


def _qk_norm_kernel(q_ref, k_ref, gq_ref, gk_ref, qo_ref, ko_ref, *, dh):
    width = q_ref.shape[1]
    row = lax.broadcasted_iota(jnp.int32, (LANE, LANE), 0)
    col = lax.broadcasted_iota(jnp.int32, (LANE, LANE), 1)
    same_group = ((row // dh) == (col // dh)).astype(BF16)

    def group_rms(x_ref, g_ref, o_ref, scale):
        for tile in range(width // LANE):
            sl = slice(tile * LANE, (tile + 1) * LANE)
            x = x_ref[:, sl].astype(F32)
            sq = x * x
            hi = sq.astype(BF16)
            lo = (sq - hi.astype(F32)).astype(BF16)
            ssum = _dot(hi, same_group) + _dot(lo, same_group)
            y = x * lax.rsqrt(ssum * (1.0 / dh) + EPS) * g_ref[:, sl]
            o_ref[:, sl] = (y * scale).astype(o_ref.dtype)

    group_rms(q_ref, gq_ref, qo_ref, dh ** -0.5 * LOG2E)
    group_rms(k_ref, gk_ref, ko_ref, 1.0)


def _qk_norm(zmain, g_dq, g_dk, width, col_q, col_k, seq):
    t = zmain.shape[0]
    dh = g_dq.shape[0]
    ts = min(NORM_ROWS, seq)
    reps = width // dh
    return pl.pallas_call(
        functools.partial(_qk_norm_kernel, dh=dh),
        grid=(t // ts,),
        in_specs=[
            pl.BlockSpec((ts, width), lambda i: (i, col_q)),
            pl.BlockSpec((ts, width), lambda i: (i, col_k)),
            pl.BlockSpec((1, width), lambda i: (0, 0)),
            pl.BlockSpec((1, width), lambda i: (0, 0)),
        ],
        out_specs=[pl.BlockSpec((ts, width), lambda i: (i, 0)),
                   pl.BlockSpec((ts, width), lambda i: (i, 0))],
        out_shape=[jax.ShapeDtypeStruct((t, width), BF16), jax.ShapeDtypeStruct((t, width), BF16)],
        compiler_params=_params("parallel"),
        name="qk_norm",
    )(zmain, zmain, jnp.tile(g_dq, reps).reshape(1, width), jnp.tile(g_dk, reps).reshape(1, width))


def _diff_attn_kernel(q_ref, k_ref, v_ref, lq1_ref, lk1_ref, lq2_ref, lk2_ref, g_ref, out_ref,
                      m_scr, l_scr, acc_scr, s_scr, *, lam_init, dh):
    tq = q_ref.shape[0]
    hw = 2 * dh
    heads = q_ref.shape[1] // hw
    i = pl.program_id(2)
    head_lanes = [slice(h * hw, (h + 1) * hw) for h in range(heads)]

    q_streams = []
    for lanes in head_lanes:
        q = q_ref[:, lanes]
        lane = lax.broadcasted_iota(jnp.int32, q.shape, 1)
        zero = jnp.zeros_like(q)
        q_streams += [jnp.where(lane < dh, q, zero), jnp.where(lane >= dh, q, zero)]

    m_scr[...] = jnp.full_like(m_scr, NEG_INF)
    l_scr[...] = jnp.zeros_like(l_scr)
    acc_scr[...] = jnp.zeros_like(acc_scr)

    def key_block(ref, j, lanes):
        return ref[pl.ds(pl.multiple_of(j * tq, tq), tq), lanes]

    def put_scores(j, slot):
        for h, lanes in enumerate(head_lanes):
            k = key_block(k_ref, j, lanes)
            for c in range(2):
                s_scr[slot, 2 * h + c] = _dot_nt(k, q_streams[2 * h + c])

    def consume(j, slot, masked):
        probs, alphas = [], []
        for u in range(2 * heads):
            s = s_scr[slot, u]
            if masked:
                key = lax.broadcasted_iota(jnp.int32, s.shape, 0)
                qry = lax.broadcasted_iota(jnp.int32, s.shape, 1)
                s = jnp.where(key <= qry, s, NEG_INF)
            m_prev = m_scr[u]
            m_new = jnp.maximum(m_prev, jnp.max(s, axis=0, keepdims=True))
            alpha = jnp.exp2(m_prev - m_new)
            p = jnp.exp2(s - m_new)
            l_scr[u] = alpha * l_scr[u] + jnp.sum(p, axis=0, keepdims=True)
            m_scr[u] = m_new
            probs.append(p.astype(BF16))
            alphas.append(alpha)
        for h, lanes in enumerate(head_lanes):
            v = key_block(v_ref, j, lanes)
            for u in (2 * h, 2 * h + 1):
                acc_scr[u] = alphas[u] * acc_scr[u] + _dot_tn(v, probs[u])

    put_scores(0, 0)

    def pair(p, carry):
        j = 2 * p
        put_scores(j + 1, 1)
        consume(j, 0, False)
        put_scores(j + 2, 0)
        consume(j + 1, 1, False)
        return carry

    lax.fori_loop(0, i // 2, pair, 0)

    @pl.when(i % 2 == 0)
    def _():
        consume(i, 0, True)

    @pl.when(i % 2 == 1)
    def _():
        put_scores(i, 1)
        consume(i - 1, 0, False)
        consume(i, 1, True)

    lam = (jnp.exp(jnp.sum(lq1_ref[...] * lk1_ref[...], axis=-1, keepdims=True))
           - jnp.exp(jnp.sum(lq2_ref[...] * lk2_ref[...], axis=-1, keepdims=True)) + lam_init)
    for h, lanes in enumerate(head_lanes):
        u = 2 * h
        o = (acc_scr[u] / l_scr[u] - lam * (acc_scr[u + 1] / l_scr[u + 1])).T
        on = o * lax.rsqrt(jnp.mean(o * o, axis=-1, keepdims=True) + EPS) * g_ref[...]
        out_ref[:, lanes] = (on * (1.0 - lam_init)).astype(out_ref.dtype)


def _diff_attn(qn, kn, zmain, col_v, lam_params, g_dsub, lam_init, batch, seq, width):
    t = qn.shape[0]
    dh = lam_params[0].shape[0]
    hw = 2 * dh
    hb = ATTN_HEADS_PER_STEP
    bw = hb * hw
    tq = min(ATTN_ROWS, seq)
    nq = seq // tq
    v_col0 = col_v * (width // bw)
    lam_specs = [pl.BlockSpec((1, dh), lambda b, h, i: (0, 0)) for _ in range(4)]
    return pl.pallas_call(
        functools.partial(_diff_attn_kernel, lam_init=lam_init, dh=dh),
        grid=(batch, width // bw, nq),
        in_specs=[
            pl.BlockSpec((tq, bw), lambda b, h, i: (b * nq + i, h)),
            pl.BlockSpec((seq, bw), lambda b, h, i: (b, h)),
            pl.BlockSpec((seq, bw), lambda b, h, i: (b, v_col0 + h)),
            *lam_specs,
            pl.BlockSpec((1, hw), lambda b, h, i: (0, 0)),
        ],
        out_specs=pl.BlockSpec((tq, bw), lambda b, h, i: (b * nq + i, h)),
        out_shape=jax.ShapeDtypeStruct((t, width), BF16),
        scratch_shapes=[
            pltpu.VMEM((2 * hb, 1, tq), F32),
            pltpu.VMEM((2 * hb, 1, tq), F32),
            pltpu.VMEM((2 * hb, hw, tq), F32),
            pltpu.VMEM((2, 2 * hb, tq, tq), F32),
        ],
        compiler_params=_params("parallel", "parallel", "arbitrary"),
        name="diff_attn",
    )(qn, kn, zmain, *[p.reshape(1, dh) for p in lam_params], g_dsub.reshape(1, hw))


def _stick_kernel(q_ref, k_ref, v_ref, out_ref, carry_scr, acc_scr, *, dh, sub, far):
    tq = q_ref.shape[0]
    heads = q_ref.shape[1] // dh
    i = pl.program_id(2)
    sign_bit = jnp.uint32(1 << 31)

    srow = lax.broadcasted_iota(jnp.int32, (sub, 2 * sub), 0)
    jcol = lax.broadcasted_iota(jnp.int32, (sub, 2 * sub), 1)
    later = ((jcol & (sub - 1)) > srow).astype(BF16)

    def softplus2(z2):
        neg_abs = pltpu.bitcast(pltpu.bitcast(z2, jnp.uint32) | sign_bit, F32)
        return jnp.maximum(z2, 0.0) + jnp.log(1.0 + jnp.exp2(neg_abs)) * LOG2E

    def suffix_in_block(sp2):
        hi = sp2.astype(BF16)
        lo = (sp2 - hi.astype(F32)).astype(BF16)
        return _dot(later, jnp.concatenate([hi, lo], axis=0))

    def own_keys(q, lanes):
        start = pl.multiple_of(i * tq, tq)
        z_tile = _dot_nt(k_ref[pl.ds(start, tq), lanes], q)
        carry = jnp.zeros((1, tq), F32)
        key_rows = []
        for c in reversed(range(tq // sub)):
            lane0 = c * sub
            z2 = z_tile[lane0:lane0 + sub, lane0:]
            before = (lax.broadcasted_iota(jnp.int32, z2.shape, 0)
                      < lax.broadcasted_iota(jnp.int32, z2.shape, 1))
            sp2 = jnp.where(before, softplus2(z2), 0.0)
            a = jnp.exp2(z2 - sp2 - (suffix_in_block(sp2) + carry[:, lane0:]))
            a = jnp.where(before, a, 0.0).astype(BF16)
            total = jnp.sum(sp2, axis=0, keepdims=True)
            if lane0:
                a = jnp.concatenate([jnp.zeros((sub, lane0), BF16), a], axis=1)
                total = jnp.concatenate([jnp.zeros((1, lane0), F32), total], axis=1)
            key_rows.insert(0, a)
            carry = carry + total
        return _dot_tn(v_ref[pl.ds(start, tq), lanes], jnp.concatenate(key_rows, axis=0)), carry

    def far_keys(q, lanes, jb, carry, valid):
        start = pl.multiple_of(jb * far, far)
        z2 = _dot_nt(k_ref[pl.ds(start, far), lanes], q)
        sp2 = softplus2(z2)
        if valid is not None:
            sp2 = jnp.where(valid, sp2, 0.0)
        parts = []
        for c in reversed(range(far // sub)):
            rows = slice(c * sub, (c + 1) * sub)
            a = jnp.exp2(z2[rows] - sp2[rows] - (suffix_in_block(sp2[rows]) + carry))
            if valid is not None:
                a = jnp.where(valid, a, 0.0)
            parts.insert(0, a.astype(BF16))
            carry = carry + jnp.sum(sp2[rows], axis=0, keepdims=True)
        return _dot_tn(v_ref[pl.ds(start, far), lanes], jnp.concatenate(parts, axis=0)), carry

    jb = i * (tq // far) - 1
    min_carry = []
    for h in range(heads):
        lanes = slice(h * dh, (h + 1) * dh)
        q = q_ref[:, lanes]
        acc, carry = own_keys(q, lanes)
        near, carry = far_keys(q, lanes, jnp.maximum(jb, 0), carry, i > 0)
        acc_scr[h] = acc + near
        carry_scr[h] = carry
        min_carry.append(jnp.min(carry))

    def keep_going(state):
        jb, min_carry = state
        return jnp.logical_and(jb >= 0, min_carry <= STICK_DEAD_LOG2)

    for h in range(heads):
        lanes = slice(h * dh, (h + 1) * dh)

        def body(state, h=h, lanes=lanes):
            jb, _ = state
            more, carry = far_keys(q_ref[:, lanes], lanes, jb, carry_scr[h], None)
            acc_scr[h] += more
            carry_scr[h] = carry
            return jb - 1, jnp.min(carry)

        lax.while_loop(keep_going, body, (jb - 1, min_carry[h]))
        out_ref[:, lanes] = acc_scr[h].T.astype(out_ref.dtype)


def _stick_breaking(zmain, col_q, col_k, col_v, batch, seq, width):
    t = zmain.shape[0]
    dh = width // SB_HEADS
    hb = ATTN_HEADS_PER_STEP
    tq = min(ATTN_ROWS, seq)
    nq = seq // tq
    per = width // (hb * dh)
    return pl.pallas_call(
        functools.partial(_stick_kernel, dh=dh, sub=LANE, far=min(STICK_FAR_KEYS, seq)),
        grid=(batch, SB_HEADS // hb, nq),
        in_specs=[
            pl.BlockSpec((tq, hb * dh), lambda b, h, i: (b * nq + i, col_q * per + h)),
            pl.BlockSpec((seq, hb * dh), lambda b, h, i: (b, col_k * per + h)),
            pl.BlockSpec((seq, hb * dh), lambda b, h, i: (b, col_v * per + h)),
        ],
        out_specs=pl.BlockSpec((tq, hb * dh), lambda b, h, i: (b * nq + i, h)),
        out_shape=jax.ShapeDtypeStruct((t, width), BF16),
        scratch_shapes=[
            pltpu.VMEM((hb, 1, tq), F32),
            pltpu.VMEM((hb, dh, tq), F32),
        ],
        compiler_params=_params("parallel", "parallel", "arbitrary"),
        name="stick_breaking",
    )(zmain, zmain, zmain)


def _merge_kernel(om_ref, od_ref, os_ref, wm_ref, wd_ref, ws_ref, gm_ref, gd_ref, gs_ref, o_ref, w_scr):
    @pl.when(pl.program_id(1) == 0)
    def _():
        for br, w_ref in enumerate((wm_ref, wd_ref, ws_ref)):
            w_scr[br] = w_ref[...].astype(BF16)

    acc = jax.nn.sigmoid(gm_ref[...].astype(F32)) * _dot(om_ref[...], w_scr[0])
    acc = acc + jax.nn.sigmoid(gd_ref[...].astype(F32)) * _dot(od_ref[...], w_scr[1])
    acc = acc + jax.nn.sigmoid(gs_ref[...].astype(F32)) * _dot(os_ref[...], w_scr[2])
    o_ref[...] = acc.astype(o_ref.dtype)


def _merge(out_m, out_d, out_s, w_branch, layer, zmain, gate_col0, d_model):
    t, width = out_m.shape
    tm = min(PROJ_ROWS, t)
    tn = HALF_COLS
    nj = d_model // tn
    act_spec = pl.BlockSpec((tm, width), lambda j, i: (i, 0))
    w_spec = lambda br: pl.BlockSpec((None, None, width, tn), lambda j, i: (layer, br, 0, j))
    gate_blk0 = gate_col0 // tn
    gate_spec = lambda br: pl.BlockSpec((tm, tn), lambda j, i: (i, gate_blk0 + br * nj + j))
    return pl.pallas_call(
        _merge_kernel,
        grid=(nj, t // tm),
        in_specs=[act_spec, act_spec, act_spec, w_spec(0), w_spec(1), w_spec(2),
                  gate_spec(0), gate_spec(1), gate_spec(2)],
        out_specs=pl.BlockSpec((tm, tn), lambda j, i: (i, j)),
        out_shape=jax.ShapeDtypeStruct((t, d_model), BF16),
        scratch_shapes=[pltpu.VMEM((3, width, tn), BF16)],
        compiler_params=_params("parallel", "arbitrary"),
        name="branch_merge",
    )(out_m, out_d, out_s, w_branch, w_branch, w_branch, zmain, zmain, zmain)


def _proj_residual_kernel(a_ref, w_ref, x_ref, gt_ref, o_ref, w_scr):
    @pl.when(pl.program_id(1) == 0)
    def _():
        w_scr[...] = w_ref[...].astype(BF16)

    o_ref[...] = x_ref[...] + gt_ref[...] * _dot(a_ref[...], w_scr[...])


def _proj_residual(a, w, layer, x2, mod, j_gate, seq, tm, tn, name):
    t, k = a.shape
    n = w.shape[2]
    tm = min(tm, seq)
    per_b = seq // tm
    return pl.pallas_call(
        _proj_residual_kernel,
        grid=(n // tn, t // tm),
        in_specs=[
            pl.BlockSpec((tm, k), lambda j, i: (i, 0)),
            pl.BlockSpec((None, k, tn), lambda j, i: (layer, 0, j)),
            pl.BlockSpec((tm, tn), lambda j, i: (i, j)),
            pl.BlockSpec((None, None, 1, tn), lambda j, i: (i // per_b, j_gate, 0, j)),
        ],
        out_specs=pl.BlockSpec((tm, tn), lambda j, i: (i, j)),
        out_shape=jax.ShapeDtypeStruct((t, n), F32),
        scratch_shapes=[pltpu.VMEM((k, tn), BF16)],
        compiler_params=_params("parallel", "arbitrary"),
        name=name,
    )(a, w, x2, mod)


def _ffn_up_kernel(h_ref, wg_ref, wu_ref, wc_ref, o_ref, tail_scr, wg_scr, wu_scr, *, tiles_per_seq):
    tm = h_ref.shape[0]
    i = pl.program_id(1)

    @pl.when(i == 0)
    def _():
        wg_scr[...] = wg_ref[...].astype(BF16)
        wu_scr[...] = wu_ref[...].astype(BF16)

    @pl.when(i % tiles_per_seq == 0)
    def _():
        tail_scr[...] = jnp.zeros_like(tail_scr)

    h = h_ref[...]
    gate = _dot(h, wg_scr[...])
    up = _dot(h, wu_scr[...])
    wc = wc_ref[...]
    xp = jnp.concatenate([tail_scr[...], gate], axis=0)
    conv = gate * wc[FF_CONV - 1:FF_CONV]
    for back in range(1, FF_CONV):
        conv = conv + pltpu.roll(xp, back, 0)[CONV_HALO:] * wc[FF_CONV - 1 - back:FF_CONV - back]
    tail_scr[...] = gate[tm - CONV_HALO:]
    o_ref[...] = (_silu(conv) * up).astype(o_ref.dtype)


def _ffn_up(h2, w_up, w_ffconv, layer, seq):
    t, k = h2.shape
    d_ff = w_ffconv.shape[2]
    tm = min(PROJ_ROWS, seq)
    tn = HALF_COLS
    nj = d_ff // tn
    return pl.pallas_call(
        functools.partial(_ffn_up_kernel, tiles_per_seq=seq // tm),
        grid=(nj, t // tm),
        in_specs=[
            pl.BlockSpec((tm, k), lambda j, i: (i, 0)),
            pl.BlockSpec((None, k, tn), lambda j, i: (layer, 0, j)),
            pl.BlockSpec((None, k, tn), lambda j, i: (layer, 0, nj + j)),
            pl.BlockSpec((None, FF_CONV, tn), lambda j, i: (layer, 0, j)),
        ],
        out_specs=pl.BlockSpec((tm, tn), lambda j, i: (i, j)),
        out_shape=jax.ShapeDtypeStruct((t, d_ff), BF16),
        scratch_shapes=[pltpu.VMEM((CONV_HALO, tn), F32), pltpu.VMEM((k, tn), BF16),
                        pltpu.VMEM((k, tn), BF16)],
        compiler_params=_params("parallel", "arbitrary"),
        name="ffn_up",
    )(h2, w_up, w_up, w_ffconv)


def kernel(x, c, w_ada, b_ada, g_mix, g_ffn, w_in, b_gate_if, w_mconv, g_mout, g_dq, g_dk,
           lam_q1, lam_k1, lam_q2, lam_k2, g_dsub, w_branch, w_out, w_up, w_ffconv, w_down):
    batch, seq, d = x.shape
    depth = w_ada.shape[0]
    width = d // 2
    t = batch * seq
    n_gate = 2 * M_HEADS

    mod_all = _ada_mod(c, w_ada, b_ada)
    x2 = x.reshape(t, d)
    w_in_t = jnp.swapaxes(w_in, 1, 2)

    for l in range(depth):
        mod = mod_all[l]
        gate0 = 4 * width
        sb_scale = (width // SB_HEADS) ** -0.5 * LOG2E
        b_if = jnp.zeros((1, LANE), F32).at[0, :n_gate].set(b_gate_if[l])

        h, zif = _norm_mod_gates(x2, g_mix[l], mod, 1, 0, seq, w_in_t, l, gate0, b_if, n_gate)
        zmain = _in_proj(h, w_in_t, l, width, n_gate, 7, sb_scale)

        out_m = _mlstm(zmain, zif, w_mconv[l], g_mout[l], batch, seq, width)
        qn, kn = _qk_norm(zmain, g_dq[l], g_dk[l], width, 4, 5, seq)
        lam_init = 0.8 - 0.6 * math.exp(-0.3 * l)
        out_d = _diff_attn(qn, kn, zmain, 6, (lam_q1[l], lam_k1[l], lam_q2[l], lam_k2[l]),
                           g_dsub[l], lam_init, batch, seq, width)
        out_s = _stick_breaking(zmain, 7, 8, 9, batch, seq, width)

        merged = _merge(out_m, out_d, out_s, w_branch, l, zmain, 10 * width, d)
        x2 = _proj_residual(merged, w_out, l, x2, mod, 2, seq, PROJ_ROWS, PROJ_COLS, "out_proj")

        h2 = _norm_mod(x2, g_ffn[l], mod, 4, 3, seq)
        act = _ffn_up(h2, w_up, w_ffconv, l, seq)
        x2 = _proj_residual(act, w_down, l, x2, mod, 5, seq, FFN_DOWN_ROWS, HALF_COLS, "ffn_down")

    return x2.reshape(batch, seq, d)
```
